```python
import math
import jax, jax.numpy as jnp
from jax import lax
import numpy as np

D_MODEL = 4096
BATCH = 4
SEQ = 4096
DEPTH = 1

D_MIX = D_MODEL
D_SSM = D_MIX // 2
D_SC = D_MIX - D_SSM
SSM_HEAD_DIM = 64
SSM_HEADS = D_SSM // SSM_HEAD_DIM
SSM_GROUPS = 8
SSM_STATE = 128
SSM_CONV = 5
SSM_CHUNK = 128
SC_CONV = 3
SC_GROUPS = 16
D_XBC = D_SSM + 2 * SSM_GROUPS * SSM_STATE
D_IN = D_SSM + D_XBC + 2 * SSM_HEADS + 3 * D_SC
D_FF = 4 * D_MODEL
N_MOD = 6
DEEPNORM_ALPHA = (2 * DEPTH) ** 0.25
DEEPNORM_BETA = (8 * DEPTH) ** -0.25
DT_PROJ_SCALE = 0.1
LN_EPS = 1e-5
RMS_EPS = 1e-5

kernel_name = "hymba_ssd_shortconv_deepnorm_adaln_encoder"


def layer_norm(x, g, b):
    xf = x.astype(jnp.float32)
    mu = jnp.mean(xf, axis=-1, keepdims=True)
    var = jnp.mean(jnp.square(xf - mu), axis=-1, keepdims=True)
    return ((xf - mu) * lax.rsqrt(var + LN_EPS) * g + b).astype(x.dtype)


def group_rms_norm(y, w, n_groups):
    bsz, s, d = y.shape
    yf = y.astype(jnp.float32).reshape(bsz, s, n_groups, d // n_groups)
    yf = yf * lax.rsqrt(jnp.mean(yf * yf, axis=-1, keepdims=True) + RMS_EPS)
    return (yf.reshape(bsz, s, d) * w).astype(y.dtype)


def dwconv_centred(u, w):
    k_w, ch = w.shape
    return lax.conv_general_dilated(
        u, w[:, None, :].astype(u.dtype), window_strides=(1,),
        padding=[(k_w // 2, k_w // 2)], dimension_numbers=('NWC', 'WIO', 'NWC'),
        feature_group_count=ch)


def ssd_chunked(x, dt, a, b_in, c_in):
    bsz, s, h, p = x.shape
    g, n = b_in.shape[2], b_in.shape[3]
    r = h // g
    q = SSM_CHUNK
    nc = s // q
    xc = x.reshape(bsz, nc, q, g, r, p)
    dtc = dt.reshape(bsz, nc, q, g, r)
    bc = b_in.reshape(bsz, nc, q, g, n)
    cc = c_in.reshape(bsz, nc, q, g, n)
    a_cum = jnp.cumsum(dtc * a.reshape(g, r), axis=2)
    xdt = xc * dtc[..., None]
    lower = jnp.tril(jnp.ones((q, q), dtype=bool))[:, :, None, None]
    seg = a_cum[:, :, :, None] - a_cum[:, :, None, :]
    decay = jnp.exp(jnp.where(lower, seg, -jnp.inf))
    scores = jnp.einsum('bcign,bcjgn->bcijg', cc, bc)
    y_diag = jnp.einsum('bcijgr,bcjgrp->bcigrp', scores[..., None] * decay, xdt)
    to_end = jnp.exp(a_cum[:, :, -1:] - a_cum)
    states = jnp.einsum('bclgn,bclgrp->bcgrpn', bc, xdt * to_end[..., None])
    chunk_decay = jnp.exp(a_cum[:, :, -1])

    def step(carry, inp):
        st, dec = inp
        return carry * dec[..., None, None] + st, carry

    init = jnp.zeros((bsz, g, r, p, n), dtype=states.dtype)
    _, prev = lax.scan(step, init, (jnp.moveaxis(states, 1, 0), jnp.moveaxis(chunk_decay, 1, 0)))
    prev = jnp.moveaxis(prev, 0, 1)
    y_off = jnp.einsum('bclgn,bcgrpn->bclgrp', cc, prev) * jnp.exp(a_cum)[..., None]
    return (y_diag + y_off).reshape(bsz, s, h, p).astype(x.dtype)


def ssd_mixer(u_z, u_xbc, u_dt, conv_w, conv_b, dt_bias_f, dt_bias_b, a_log_f, a_log_b, d_skip, norm_w):
    bsz, s, _ = u_z.shape
    xbc = jax.nn.silu(dwconv_centred(u_xbc, conv_w) + conv_b)
    xs, bs, cs = jnp.split(xbc, [D_SSM, D_SSM + SSM_GROUPS * SSM_STATE], axis=-1)
    xs = xs.reshape(bsz, s, SSM_HEADS, SSM_HEAD_DIM)
    bs = bs.reshape(bsz, s, SSM_GROUPS, SSM_STATE)
    cs = cs.reshape(bsz, s, SSM_GROUPS, SSM_STATE)
    dt = u_dt.astype(jnp.float32)
    dt_f = jax.nn.softplus(dt[..., :SSM_HEADS] + dt_bias_f)
    dt_b = jax.nn.softplus(dt[..., SSM_HEADS:] + dt_bias_b)
    a_f = -jnp.exp(a_log_f.astype(jnp.float32))
    a_b = -jnp.exp(a_log_b.astype(jnp.float32))
    flip = lambda t: jnp.flip(t, axis=1)
    y_f = ssd_chunked(xs, dt_f, a_f, bs, cs)
    y_b = flip(ssd_chunked(flip(xs), flip(dt_b), a_b, flip(bs), flip(cs)))
    y = y_f + y_b + d_skip[:, None] * xs
    y = y.reshape(bsz, s, D_SSM) * jax.nn.silu(u_z)
    return group_rms_norm(y, norm_w, SSM_GROUPS)


def short_conv_mixer(u_h, u_b, u_c, conv_w, norm_w):
    y = u_b * dwconv_centred(u_c * u_h, conv_w)
    return group_rms_norm(y, norm_w, SC_GROUPS)


def setup_inputs(seed: int = 0) -> dict:
    key = jax.random.key(seed)
    ks = jax.random.split(key, 24)
    nrm = jax.random.normal
    x = nrm(ks[0], (BATCH, SEQ, D_MODEL), jnp.float32)
    c = nrm(ks[1], (BATCH, D_MODEL), jnp.float32)
    w_ada = nrm(ks[2], (DEPTH, D_MODEL, N_MOD * D_MODEL), jnp.float32) * (0.1 * D_MODEL ** -0.5)
    b_ada = 0.01 * nrm(ks[3], (DEPTH, N_MOD * D_MODEL), jnp.float32)
    dt_lo = D_SSM + D_XBC
    col_scale = jnp.ones((D_IN,), jnp.float32).at[dt_lo:dt_lo + 2 * SSM_HEADS].set(DT_PROJ_SCALE)
    w_in = nrm(ks[4], (DEPTH, D_MODEL, D_IN), jnp.float32) * (D_MODEL ** -0.5) * col_scale
    ssm_conv_w = nrm(ks[5], (DEPTH, SSM_CONV, D_XBC), jnp.float32) * SSM_CONV ** -0.5
    ssm_conv_b = 0.01 * nrm(ks[6], (DEPTH, D_XBC), jnp.float32)

    def dt_bias(k):
        dt0 = jnp.exp(jax.random.uniform(k, (DEPTH, SSM_HEADS), jnp.float32, math.log(1e-3), math.log(1e-1)))
        return dt0 + jnp.log(-jnp.expm1(-dt0))

    ssm_dt_bias_f = dt_bias(ks[7])
    ssm_dt_bias_b = dt_bias(ks[8])
    ssm_a_log_f = jnp.log(jax.random.uniform(ks[9], (DEPTH, SSM_HEADS), jnp.float32, 1.0, 16.0))
    ssm_a_log_b = jnp.log(jax.random.uniform(ks[10], (DEPTH, SSM_HEADS), jnp.float32, 1.0, 16.0))
    ssm_d = 1.0 + 0.1 * nrm(ks[11], (DEPTH, SSM_HEADS), jnp.float32)
    ssm_norm_w = 1.0 + 0.02 * nrm(ks[12], (DEPTH, D_SSM), jnp.float32)
    sc_conv_w = nrm(ks[13], (DEPTH, SC_CONV, D_SC), jnp.float32) * SC_CONV ** -0.5
    sc_norm_w = 1.0 + 0.02 * nrm(ks[14], (DEPTH, D_SC), jnp.float32)
    w_out = nrm(ks[15], (DEPTH, D_MIX, D_MODEL), jnp.float32) * (D_MIX ** -0.5 * DEEPNORM_BETA)
    ln1_g = 1.0 + 0.02 * nrm(ks[16], (DEPTH, D_MODEL), jnp.float32)
    ln1_b = 0.01 * nrm(ks[17], (DEPTH, D_MODEL), jnp.float32)
    w_up = nrm(ks[18], (DEPTH, D_MODEL, D_FF), jnp.float32) * D_MODEL ** -0.5
    w_down = nrm(ks[19], (DEPTH, D_FF, D_MODEL), jnp.float32) * (D_FF ** -0.5 * DEEPNORM_BETA)
    ln2_g = 1.0 + 0.02 * nrm(ks[20], (DEPTH, D_MODEL), jnp.float32)
    ln2_b = 0.01 * nrm(ks[21], (DEPTH, D_MODEL), jnp.float32)
    return {"x": x, "c": c, "w_ada": w_ada, "b_ada": b_ada, "w_in": w_in,
            "ssm_conv_w": ssm_conv_w, "ssm_conv_b": ssm_conv_b,
            "ssm_dt_bias_f": ssm_dt_bias_f, "ssm_dt_bias_b": ssm_dt_bias_b,
            "ssm_a_log_f": ssm_a_log_f, "ssm_a_log_b": ssm_a_log_b, "ssm_d": ssm_d,
            "ssm_norm_w": ssm_norm_w, "sc_conv_w": sc_conv_w, "sc_norm_w": sc_norm_w,
            "w_out": w_out, "ln1_g": ln1_g, "ln1_b": ln1_b, "w_up": w_up, "w_down": w_down,
            "ln2_g": ln2_g, "ln2_b": ln2_b}


def reference(x, c, w_ada, b_ada, w_in, ssm_conv_w, ssm_conv_b, ssm_dt_bias_f, ssm_dt_bias_b,
              ssm_a_log_f, ssm_a_log_b, ssm_d, ssm_norm_w, sc_conv_w, sc_norm_w, w_out,
              ln1_g, ln1_b, w_up, w_down, ln2_g, ln2_b):
    bounds = [int(v) for v in np.cumsum([D_SSM, D_XBC, 2 * SSM_HEADS, D_SC, D_SC])]
    for l in range(DEPTH):
        mod = jnp.einsum('bd,dm->bm', jax.nn.silu(c), w_ada[l]) + b_ada[l]
        shift1, scale1, gate1, shift2, scale2, gate2 = jnp.split(mod[:, None, :], N_MOD, axis=-1)
        h = x * (1.0 + scale1) + shift1
        proj = jnp.einsum('bsd,de->bse', h, w_in[l])
        u_z, u_xbc, u_dt, u_h, u_b, u_c = jnp.split(proj, bounds, axis=-1)
        y_ssm = ssd_mixer(u_z, u_xbc, u_dt, ssm_conv_w[l], ssm_conv_b[l], ssm_dt_bias_f[l],
                          ssm_dt_bias_b[l], ssm_a_log_f[l], ssm_a_log_b[l], ssm_d[l], ssm_norm_w[l])
        y_sc = short_conv_mixer(u_h, u_b, u_c, sc_conv_w[l], sc_norm_w[l])
        mix = jnp.einsum('bse,ed->bsd', jnp.concatenate([y_ssm, y_sc], axis=-1), w_out[l])
        x = layer_norm(DEEPNORM_ALPHA * x + (1.0 + gate1) * mix, ln1_g[l], ln1_b[l])
        h = x * (1.0 + scale2) + shift2
        ff = jnp.square(jax.nn.relu(jnp.einsum('bsd,df->bsf', h, w_up[l])))
        ff = jnp.einsum('bsf,fd->bsd', ff, w_down[l])
        x = layer_norm(DEEPNORM_ALPHA * x + (1.0 + gate2) * ff, ln2_g[l], ln2_b[l])
    return x
```

```python
import functools

import jax
import jax.numpy as jnp
from jax import lax
from jax.experimental import pallas as pl
from jax.experimental.pallas import tpu as pltpu

F32 = jnp.float32
BF16 = jnp.bfloat16
HIGHEST = lax.Precision.HIGHEST

D_MODEL = 4096
D_SSM = 2048
D_SC = 2048
HEAD_DIM = 64
N_HEADS = 32
N_GROUPS = 8
HEADS_PER_GROUP = 4
D_STATE = 128
CHUNK = 128
SSM_CONV = 5
SC_CONV = 3
SC_GROUPS = 16
D_XBC = D_SSM + 2 * N_GROUPS * D_STATE
D_FF = 4 * D_MODEL
N_MOD = 6
LN_EPS = 1e-5
RMS_EPS = 1e-5
GROUP_W = HEADS_PER_GROUP * HEAD_DIM
D_MAIN = D_XBC + D_SSM + 3 * D_SC
DT_PAD = 128
HALO = 8
LN_ROWS = 64

COL_Z = 2
COL_H = 3
COL_B = 4
COL_C = 5

VMEM_LIMIT = 56 * 1024 * 1024


def _cparams(sem):
    return pltpu.CompilerParams(dimension_semantics=sem, vmem_limit_bytes=VMEM_LIMIT)


def _sigmoid(v):
    return 1.0 / (1.0 + jnp.exp(-v))


def _silu(v):
    return v * _sigmoid(v)


def _softplus(v):
    return jnp.maximum(v, 0.0) + jnp.log1p(jnp.exp(-jnp.abs(v)))


def _layer_norm(r, g, b):
    mu = jnp.mean(r, axis=-1, keepdims=True)
    d = r - mu
    var = jnp.mean(d * d, axis=-1, keepdims=True)
    return d * lax.rsqrt(var + LN_EPS) * g + b


def _ada_body(c_ref, w_ref, b_ref, o_ref):
    o_ref[...] = jnp.dot(_silu(c_ref[...]), w_ref[...], preferred_element_type=F32,
                         precision=HIGHEST) + b_ref[...]


def _ada(c_pad, w_ada, b_ada):
    rows, d = c_pad.shape
    n = w_ada.shape[1]
    tn = 512
    return pl.pallas_call(
        _ada_body,
        grid=(n // tn,),
        in_specs=[pl.BlockSpec((rows, d), lambda j: (0, 0)),
                  pl.BlockSpec((d, tn), lambda j: (0, j)),
                  pl.BlockSpec((1, tn), lambda j: (0, j))],
        out_specs=pl.BlockSpec((rows, tn), lambda j: (0, j)),
        out_shape=jax.ShapeDtypeStruct((rows, n), F32),
        compiler_params=_cparams(("arbitrary",)),
        name="ada",
    )(c_pad, w_ada, b_ada)


def _inproj_body(x_ref, sc_ref, sh_ref, w_ref, wdt_ref, p_ref, dt_ref, h_scr):
    @pl.when(pl.program_id(1) == 0)
    def _():
        h = (x_ref[...] * (1.0 + sc_ref[0]) + sh_ref[0]).astype(BF16)
        h_scr[...] = h
        dt_ref[...] = jnp.dot(h, wdt_ref[...], preferred_element_type=F32)

    p_ref[...] = jnp.dot(h_scr[...], w_ref[...], preferred_element_type=F32)


def _inproj(x2, scale, shift, w_main, w_dt, seq):
    m, d = x2.shape
    n = w_main.shape[1]
    tm, tn = 512, 1024
    tiles_per_batch = seq // tm
    return pl.pallas_call(
        _inproj_body,
        grid=(m // tm, n // tn),
        in_specs=[pl.BlockSpec((tm, d), lambda i, j: (i, 0)),
                  pl.BlockSpec((1, 1, d), lambda i, j: (i // tiles_per_batch, 0, 0)),
                  pl.BlockSpec((1, 1, d), lambda i, j: (i // tiles_per_batch, 0, 0)),
                  pl.BlockSpec((d, tn), lambda i, j: (0, j)),
                  pl.BlockSpec((d, DT_PAD), lambda i, j: (0, 0))],
        out_specs=[pl.BlockSpec((tm, tn), lambda i, j: (i, j)),
                   pl.BlockSpec((tm, DT_PAD), lambda i, j: (i, 0))],
        out_shape=[jax.ShapeDtypeStruct((m, n), F32),
                   jax.ShapeDtypeStruct((m, DT_PAD), F32)],
        scratch_shapes=[pltpu.VMEM((tm, d), BF16)],
        compiler_params=_cparams(("arbitrary", "arbitrary")),
        name="inproj",
    )(x2, scale, shift, w_main, w_dt)


def _expand_heads(q, e3_ref):
    hi = q.astype(BF16)
    r1 = q - hi.astype(F32)
    mid = r1.astype(BF16)
    lo = (r1 - mid.astype(F32)).astype(BF16)
    pieces = jnp.concatenate([hi, mid, lo], axis=1)
    return jnp.dot(pieces, e3_ref[...], preferred_element_type=F32)


def _chunk_decay_terms(dt_ref, bias_ref, alog_ref):
    row = lax.broadcasted_iota(jnp.int32, (CHUNK, CHUNK), 0)
    col = lax.broadcasted_iota(jnp.int32, (CHUNK, CHUNK), 1)
    tril = col <= row
    triu = col >= row
    dtq = _softplus(dt_ref[...] + bias_ref[...])
    da = dtq * (-jnp.exp(alog_ref[...]))
    pre = jnp.dot(tril.astype(F32), da, preferred_element_type=F32, precision=HIGHEST)
    suf = jnp.dot(triu.astype(F32), da, preferred_element_type=F32, precision=HIGHEST)
    acum = jnp.where(col < N_HEADS, pre, suf)
    return dtq, acum, tril, triu


def _ssd_bwd_body(cur_ref, prev_ref, next_ref, dt_ref, cw_ref, cb_ref, bias_ref, alog_ref, e3_ref,
                  act_ref, yoff_ref, st_ref, *, n_chunks):
    s = pl.program_id(1)
    c = n_chunks - 1 - s

    @pl.when(s == 0)
    def _():
        st_ref[...] = jnp.zeros_like(st_ref)

    lane_tile = 512
    row = lax.broadcasted_iota(jnp.int32, (CHUNK, lane_tile), 0)
    has_prev = c > 0
    has_next = c < n_chunks - 1
    for t in range(D_XBC // lane_tile):
        sl = slice(t * lane_tile, (t + 1) * lane_tile)
        u = cur_ref[:, sl]
        pm = jnp.where(has_prev, prev_ref[:, sl], 0.0)
        nx = jnp.where(has_next, next_ref[:, sl], 0.0)
        d1 = jnp.where(row == 0, pm[HALO - 1:HALO, :], pltpu.roll(u, 1, 0))
        d2 = jnp.where(row == 0, pm[HALO - 2:HALO - 1, :],
                       jnp.where(row == 1, pm[HALO - 1:HALO, :], pltpu.roll(u, 2, 0)))
        u1 = jnp.where(row == CHUNK - 1, nx[0:1, :], pltpu.roll(u, CHUNK - 1, 0))
        u2 = jnp.where(row == CHUNK - 2, nx[0:1, :],
                       jnp.where(row == CHUNK - 1, nx[1:2, :], pltpu.roll(u, CHUNK - 2, 0)))
        w = cw_ref[:, sl]
        conv = (w[0:1, :] * d2 + w[1:2, :] * d1 + w[2:3, :] * u + w[3:4, :] * u1 + w[4:5, :] * u2
                + cb_ref[:, sl])
        act_ref[:, sl] = _silu(conv)

    dtq, acum, _, _ = _chunk_decay_terms(dt_ref, bias_ref, alog_ref)
    lane = lax.broadcasted_iota(jnp.int32, (CHUNK, DT_PAD), 1)
    acum = jnp.where((lane >= N_HEADS) & (lane < 2 * N_HEADS), acum, 0.0)
    total = acum[0:1, :]
    eb_x = _expand_heads(jnp.exp(acum), e3_ref)
    wb_x = _expand_heads(dtq * jnp.exp(total - acum), e3_ref)
    dec_x = eb_x[0:1, :]

    for g in range(N_GROUPS):
        ch = slice(g * GROUP_W, (g + 1) * GROUP_W)
        b_g = act_ref[:, D_SSM + g * D_STATE:D_SSM + (g + 1) * D_STATE].astype(BF16)
        c_g = act_ref[:, D_SSM + N_GROUPS * D_STATE + g * D_STATE:
                      D_SSM + N_GROUPS * D_STATE + (g + 1) * D_STATE].astype(BF16)
        st = st_ref[g]
        yoff_ref[:, ch] = jnp.dot(c_g, st.astype(BF16), preferred_element_type=F32) * eb_x[:, ch]
        xw = (act_ref[:, ch] * wb_x[:, ch]).astype(BF16)
        upd = lax.dot_general(b_g, xw, (((0,), (0,)), ((), ())), preferred_element_type=F32)
        st_ref[g] = st * dec_x[:, ch] + upd


def _ssd_bwd(p_main, dt, conv_w, conv_b, bias_q, alog_q, e3b, batch, seq):
    m = p_main.shape[0]
    nc = seq // CHUNK
    rb = CHUNK // HALO
    last_halo = m // HALO - 1

    def cur_map(b, s):
        return (b * nc + (nc - 1 - s), 0)

    def prev_map(b, s):
        return (jnp.maximum((b * nc + (nc - 1 - s)) * rb - 1, 0), 0)

    def next_map(b, s):
        return (jnp.minimum((b * nc + (nc - 1 - s) + 1) * rb, last_halo), 0)

    const = lambda b, s: (0, 0)
    return pl.pallas_call(
        functools.partial(_ssd_bwd_body, n_chunks=nc),
        grid=(batch, nc),
        in_specs=[pl.BlockSpec((CHUNK, D_XBC), cur_map),
                  pl.BlockSpec((HALO, D_XBC), prev_map),
                  pl.BlockSpec((HALO, D_XBC), next_map),
                  pl.BlockSpec((CHUNK, DT_PAD), cur_map),
                  pl.BlockSpec((HALO, D_XBC), const),
                  pl.BlockSpec((1, D_XBC), const),
                  pl.BlockSpec((1, DT_PAD), const),
                  pl.BlockSpec((1, DT_PAD), const),
                  pl.BlockSpec((3 * DT_PAD, D_SSM), const)],
        out_specs=[pl.BlockSpec((CHUNK, D_XBC), cur_map),
                   pl.BlockSpec((CHUNK, D_SSM), cur_map)],
        out_shape=[jax.ShapeDtypeStruct((m, D_XBC), F32),
                   jax.ShapeDtypeStruct((m, D_SSM), F32)],
        scratch_shapes=[pltpu.VMEM((N_GROUPS, D_STATE, GROUP_W), F32)],
        compiler_params=_cparams(("arbitrary", "arbitrary")),
        name="ssd_bwd",
    )(p_main, p_main, p_main, dt, conv_w, conv_b, bias_q, alog_q, e3b)


def _ssd_fwd_body(act_ref, z_ref, dt_ref, yoffb_ref, bias_ref, alog_ref, dskip_ref, normw_ref, e3_ref,
                  y_ref, st_ref):
    @pl.when(pl.program_id(1) == 0)
    def _():
        st_ref[...] = jnp.zeros_like(st_ref)

    dtq, acum, tril, triu = _chunk_decay_terms(dt_ref, bias_ref, alog_ref)
    acum_t = acum.T
    dtq_t = dtq.T
    lane = lax.broadcasted_iota(jnp.int32, (CHUNK, DT_PAD), 1)
    acum_f = jnp.where(lane < N_HEADS, acum, 0.0)
    total = acum_f[CHUNK - 1:CHUNK, :]
    ef_x = _expand_heads(jnp.exp(acum_f), e3_ref)
    wf_x = _expand_heads(dtq * jnp.exp(total - acum_f), e3_ref)
    dec_x = ef_x[CHUNK - 1:CHUNK, :]
    head_of_lane = lax.broadcasted_iota(jnp.int32, (CHUNK, GROUP_W), 1) // HEAD_DIM
    neg_inf = jnp.float32(-jnp.inf)

    for g in range(N_GROUPS):
        ch = slice(g * GROUP_W, (g + 1) * GROUP_W)
        b_g = act_ref[:, D_SSM + g * D_STATE:D_SSM + (g + 1) * D_STATE].astype(BF16)
        c_g = act_ref[:, D_SSM + N_GROUPS * D_STATE + g * D_STATE:
                      D_SSM + N_GROUPS * D_STATE + (g + 1) * D_STATE].astype(BF16)
        x_g = act_ref[:, ch]
        scores = lax.dot_general(c_g, b_g, (((1,), (1,)), ((), ())), preferred_element_type=F32)

        mats = []
        blocks = []
        for r in range(HEADS_PER_GROUP):
            hf = g * HEADS_PER_GROUP + r
            hb = N_HEADS + hf
            seg_f = acum[:, hf:hf + 1] - acum_t[hf:hf + 1, :]
            seg_b = acum[:, hb:hb + 1] - acum_t[hb:hb + 1, :]
            lf = jnp.exp(jnp.where(tril, seg_f, neg_inf)) * dtq_t[hf:hf + 1, :]
            ub = jnp.exp(jnp.where(triu, seg_b, neg_inf)) * dtq_t[hb:hb + 1, :]
            mats.append((scores * (lf + ub)).astype(BF16))
            blocks.append(jnp.where(head_of_lane == r, x_g, 0.0).astype(BF16))
        y_diag = jnp.dot(jnp.concatenate(mats, axis=1), jnp.concatenate(blocks, axis=0),
                         preferred_element_type=F32)

        st = st_ref[g]
        y_off = jnp.dot(c_g, st.astype(BF16), preferred_element_type=F32) * ef_x[:, ch]
        y = y_diag + y_off + yoffb_ref[:, ch] + dskip_ref[:, ch] * x_g

        xw = (x_g * wf_x[:, ch]).astype(BF16)
        upd = lax.dot_general(b_g, xw, (((0,), (0,)), ((), ())), preferred_element_type=F32)
        st_ref[g] = st * dec_x[:, ch] + upd

        y = y * _silu(z_ref[:, ch])
        ms = jnp.mean(y * y, axis=-1, keepdims=True)
        y_ref[:, ch] = (y * lax.rsqrt(ms + RMS_EPS) * normw_ref[:, ch]).astype(BF16)


def _ssd_fwd(act, p_main, dt, yoffb, bias_q, alog_q, dskip_x, norm_w, e3f, batch, seq):
    m = act.shape[0]
    nc = seq // CHUNK
    cur = lambda b, s: (b * nc + s, 0)
    const = lambda b, s: (0, 0)
    return pl.pallas_call(
        _ssd_fwd_body,
        grid=(batch, nc),
        in_specs=[pl.BlockSpec((CHUNK, D_XBC), cur),
                  pl.BlockSpec((CHUNK, D_SSM), lambda b, s: (b * nc + s, COL_Z)),
                  pl.BlockSpec((CHUNK, DT_PAD), cur),
                  pl.BlockSpec((CHUNK, D_SSM), cur),
                  pl.BlockSpec((1, DT_PAD), const),
                  pl.BlockSpec((1, DT_PAD), const),
                  pl.BlockSpec((1, D_SSM), const),
                  pl.BlockSpec((1, D_SSM), const),
                  pl.BlockSpec((3 * DT_PAD, D_SSM), const)],
        out_specs=pl.BlockSpec((CHUNK, D_SSM), cur),
        out_shape=jax.ShapeDtypeStruct((m, D_SSM), BF16),
        scratch_shapes=[pltpu.VMEM((N_GROUPS, D_STATE, GROUP_W), F32)],
        compiler_params=_cparams(("arbitrary", "arbitrary")),
        name="ssd_fwd",
    )(act, p_main, dt, yoffb, bias_q, alog_q, dskip_x, norm_w, e3f)


def _shortconv_body(h_ref, b_ref, c_ref, hp_ref, cp_ref, hn_ref, cn_ref, cw_ref, nw_ref, y_ref,
                    *, tiles_per_seq, rows):
    i = pl.program_id(0)
    t = i % tiles_per_seq
    has_prev = t > 0
    has_next = t < tiles_per_seq - 1
    lane_tile = 512
    row = lax.broadcasted_iota(jnp.int32, (rows, lane_tile), 0)
    group = D_SC // SC_GROUPS
    for q in range(D_SC // lane_tile):
        sl = slice(q * lane_tile, (q + 1) * lane_tile)
        v = c_ref[:, sl] * h_ref[:, sl]
        vp = jnp.where(has_prev, cp_ref[HALO - 1:HALO, sl] * hp_ref[HALO - 1:HALO, sl], 0.0)
        vn = jnp.where(has_next, cn_ref[0:1, sl] * hn_ref[0:1, sl], 0.0)
        down = jnp.where(row == 0, vp, pltpu.roll(v, 1, 0))
        up = jnp.where(row == rows - 1, vn, pltpu.roll(v, rows - 1, 0))
        w = cw_ref[:, sl]
        y = b_ref[:, sl] * (w[0:1, :] * down + w[1:2, :] * v + w[2:3, :] * up)
        for k in range(lane_tile // group):
            gs = slice(k * group, (k + 1) * group)
            yg = y[:, gs]
            ms = jnp.mean(yg * yg, axis=-1, keepdims=True)
            lo = q * lane_tile + k * group
            y_ref[:, lo:lo + group] = (yg * lax.rsqrt(ms + RMS_EPS) * nw_ref[:, lo:lo + group]).astype(BF16)


def _shortconv(p_main, conv_w, norm_w, seq):
    m = p_main.shape[0]
    rows = 256
    tiles_per_seq = seq // rows
    rb = rows // HALO
    last_halo = m // HALO - 1

    def cur(col):
        return lambda i: (i, col)

    def prev(col):
        return lambda i: (jnp.maximum(i * rb - 1, 0), col)

    def nxt(col):
        return lambda i: (jnp.minimum((i + 1) * rb, last_halo), col)

    const = lambda i: (0, 0)
    return pl.pallas_call(
        functools.partial(_shortconv_body, tiles_per_seq=tiles_per_seq, rows=rows),
        grid=(m // rows,),
        in_specs=[pl.BlockSpec((rows, D_SC), cur(COL_H)),
                  pl.BlockSpec((rows, D_SC), cur(COL_B)),
                  pl.BlockSpec((rows, D_SC), cur(COL_C)),
                  pl.BlockSpec((HALO, D_SC), prev(COL_H)),
                  pl.BlockSpec((HALO, D_SC), prev(COL_C)),
                  pl.BlockSpec((HALO, D_SC), nxt(COL_H)),
                  pl.BlockSpec((HALO, D_SC), nxt(COL_C)),
                  pl.BlockSpec((HALO, D_SC), const),
                  pl.BlockSpec((1, D_SC), const)],
        out_specs=pl.BlockSpec((rows, D_SC), lambda i: (i, 0)),
        out_shape=jax.ShapeDtypeStruct((m, D_SC), BF16),
        compiler_params=_cparams(("arbitrary",)),
        name="shortconv",
    )(p_main, p_main, p_main, p_main, p_main, p_main, p_main, conv_w, norm_w)


def _residual_layer_norm(o_ref, x_ref, gate_ref, lng_ref, lnb_ref, alpha):
    rows = o_ref.shape[0]
    for r0 in range(0, rows, LN_ROWS):
        sl = slice(r0, r0 + LN_ROWS)
        r = alpha * x_ref[sl, :] + (1.0 + gate_ref[0]) * o_ref[sl, :]
        o_ref[sl, :] = _layer_norm(r, lng_ref[...], lnb_ref[...])


def _outproj_body(ys_ref, yc_ref, w_ref, x_ref, gate_ref, lng_ref, lnb_ref, x1_ref, *, alpha, k_half):
    k = pl.program_id(1)

    @pl.when(k == 0)
    def _():
        x1_ref[...] = jnp.zeros_like(x1_ref)

    @pl.when(k < k_half)
    def _():
        x1_ref[...] += jnp.dot(ys_ref[...], w_ref[...], preferred_element_type=F32)

    @pl.when(k >= k_half)
    def _():
        x1_ref[...] += jnp.dot(yc_ref[...], w_ref[...], preferred_element_type=F32)

    @pl.when(k == 2 * k_half - 1)
    def _():
        _residual_layer_norm(x1_ref, x_ref, gate_ref, lng_ref, lnb_ref, alpha)


def _outproj(y_ssm, y_sc, w_out, x2, gate, ln_g, ln_b, seq, alpha):
    m, d = x2.shape
    tm, tk = 512, 512
    k_half = D_SSM // tk
    tiles_per_batch = seq // tm
    mod_map = lambda i, k: (i // tiles_per_batch, 0, 0)
    const = lambda i, k: (0, 0)
    return pl.pallas_call(
        functools.partial(_outproj_body, alpha=alpha, k_half=k_half),
        grid=(m // tm, 2 * k_half),
        in_specs=[pl.BlockSpec((tm, tk), lambda i, k: (i, jnp.minimum(k, k_half - 1))),
                  pl.BlockSpec((tm, tk), lambda i, k: (i, jnp.maximum(k - k_half, 0))),
                  pl.BlockSpec((tk, d), lambda i, k: (k, 0)),
                  pl.BlockSpec((tm, d), lambda i, k: (i, 0)),
                  pl.BlockSpec((1, 1, d), mod_map),
                  pl.BlockSpec((1, d), const),
                  pl.BlockSpec((1, d), const)],
        out_specs=pl.BlockSpec((tm, d), lambda i, k: (i, 0)),
        out_shape=jax.ShapeDtypeStruct((m, d), F32),
        compiler_params=_cparams(("arbitrary", "arbitrary")),
        name="outproj",
    )(y_ssm, y_sc, w_out, x2, gate, ln_g, ln_b)


def _ffn_body(x1_ref, sc_ref, sh_ref, gate_ref, wu_ref, wd_ref, lng_ref, lnb_ref, o_ref, h_scr,
              *, alpha):
    f = pl.program_id(1)

    @pl.when(f == 0)
    def _():
        h_scr[...] = (x1_ref[...] * (1.0 + sc_ref[0]) + sh_ref[0]).astype(BF16)
        o_ref[...] = jnp.zeros_like(o_ref)

    u = jnp.maximum(jnp.dot(h_scr[...], wu_ref[...], preferred_element_type=F32), 0.0)
    o_ref[...] += jnp.dot((u * u).astype(BF16), wd_ref[...], preferred_element_type=F32)

    @pl.when(f == pl.num_programs(1) - 1)
    def _():
        _residual_layer_norm(o_ref, x1_ref, gate_ref, lng_ref, lnb_ref, alpha)


def _ffn(x1, scale, shift, gate, w_up, w_down, ln_g, ln_b, seq, alpha):
    m, d = x1.shape
    ff = w_up.shape[1]
    tm, tf = 512, 256
    tiles_per_batch = seq // tm
    mod_map = lambda i, f: (i // tiles_per_batch, 0, 0)
    const = lambda i, f: (0, 0)
    return pl.pallas_call(
        functools.partial(_ffn_body, alpha=alpha),
        grid=(m // tm, ff // tf),
        in_specs=[pl.BlockSpec((tm, d), lambda i, f: (i, 0)),
                  pl.BlockSpec((1, 1, d), mod_map),
                  pl.BlockSpec((1, 1, d), mod_map),
                  pl.BlockSpec((1, 1, d), mod_map),
                  pl.BlockSpec((d, tf), lambda i, f: (0, f)),
                  pl.BlockSpec((tf, d), lambda i, f: (f, 0)),
                  pl.BlockSpec((1, d), const),
                  pl.BlockSpec((1, d), const)],
        out_specs=pl.BlockSpec((tm, d), lambda i, f: (i, 0)),
        out_shape=jax.ShapeDtypeStruct((m, d), F32),
        scratch_shapes=[pltpu.VMEM((tm, d), BF16)],
        compiler_params=_cparams(("arbitrary", "arbitrary")),
        name="ffn",
    )(x1, scale, shift, gate, w_up, w_down, ln_g, ln_b)


def _head_expansion(lane_offset):
    rows = jnp.arange(DT_PAD)[:, None]
    heads = jnp.arange(D_SSM)[None, :] // HEAD_DIM
    e = (rows == heads + lane_offset).astype(BF16)
    return jnp.tile(e, (3, 1))


def _pad_rows(a, rows):
    return jnp.pad(a, ((0, rows - a.shape[0]), (0, 0)))


def kernel(x, c, w_ada, b_ada, w_in, ssm_conv_w, ssm_conv_b, ssm_dt_bias_f, ssm_dt_bias_b, ssm_a_log_f, ssm_a_log_b, ssm_d, ssm_norm_w, sc_conv_w, sc_norm_w, w_out, ln1_g, ln1_b, w_up, w_down, ln2_g, ln2_b):
    batch, seq, d = x.shape
    depth = w_ada.shape[0]
    alpha = float((2 * depth) ** 0.25)
    m = batch * seq
    dt_lo = D_SSM + D_XBC
    dt_hi = dt_lo + 2 * N_HEADS
    e3f = _head_expansion(0)
    e3b = _head_expansion(N_HEADS)
    c_pad = _pad_rows(c, HALO)
    x2 = x.reshape(m, d)

    for l in range(depth):
        mod = _ada(c_pad, w_ada[l], b_ada[l][None, :])[:batch]
        shift1, scale1, gate1, shift2, scale2, gate2 = [
            t[:, None, :] for t in jnp.split(mod, N_MOD, axis=-1)]

        w = w_in[l]
        w_main = jnp.concatenate([w[:, D_SSM:dt_lo], w[:, :D_SSM], w[:, dt_hi:]], axis=1).astype(BF16)
        w_dt = jnp.pad(w[:, dt_lo:dt_hi], ((0, 0), (0, DT_PAD - 2 * N_HEADS))).astype(BF16)
        p_main, dt = _inproj(x2, scale1, shift1, w_main, w_dt, seq)

        zeros_q = jnp.zeros((DT_PAD - 2 * N_HEADS,), F32)
        bias_q = jnp.concatenate([ssm_dt_bias_f[l], ssm_dt_bias_b[l], zeros_q])[None, :]
        alog_q = jnp.concatenate([ssm_a_log_f[l], ssm_a_log_b[l], zeros_q])[None, :]
        act, yoffb = _ssd_bwd(p_main, dt, _pad_rows(ssm_conv_w[l], HALO), ssm_conv_b[l][None, :],
                              bias_q, alog_q, e3b, batch, seq)
        dskip_x = jnp.repeat(ssm_d[l], HEAD_DIM)[None, :]
        y_ssm = _ssd_fwd(act, p_main, dt, yoffb, bias_q, alog_q, dskip_x, ssm_norm_w[l][None, :], e3f,
                         batch, seq)
        y_sc = _shortconv(p_main, _pad_rows(sc_conv_w[l], HALO), sc_norm_w[l][None, :], seq)

        x2 = _outproj(y_ssm, y_sc, w_out[l].astype(BF16), x2, gate1, ln1_g[l][None, :], ln1_b[l][None, :],
                      seq, alpha)
        x2 = _ffn(x2, scale2, shift2, gate2, w_up[l].astype(BF16), w_down[l].astype(BF16),
                  ln2_g[l][None, :], ln2_b[l][None, :], seq, alpha)
    return x2.reshape(batch, seq, d)
```

```python
import functools

import jax
import jax.numpy as jnp
from jax import lax
from jax.experimental import pallas as pl
from jax.experimental.pallas import tpu as pltpu

F32 = jnp.float32
BF16 = jnp.bfloat16
HIGHEST = lax.Precision.HIGHEST

D_MODEL = 4096
D_SSM = 2048
D_SC = 2048
HEAD_DIM = 64
N_HEADS = 32
N_GROUPS = 8
HEADS_PER_GROUP = 4
D_STATE = 128
CHUNK = 128
SSM_CONV = 5
SC_CONV = 3
SC_GROUPS = 16
D_XBC = D_SSM + 2 * N_GROUPS * D_STATE
D_FF = 4 * D_MODEL
N_MOD = 6
LN_EPS = 1e-5
RMS_EPS = 1e-5
GROUP_W = HEADS_PER_GROUP * HEAD_DIM
D_MAIN = D_XBC + D_SSM + 3 * D_SC
DT_PAD = 128
HALO = 8
LN_ROWS = 64
ACC_COLS = 512

COL_Z = 2
COL_H = 3
COL_B = 4
COL_C = 5

VMEM_LIMIT = 60 * 1024 * 1024


def _cparams(sem):
    return pltpu.CompilerParams(dimension_semantics=sem, vmem_limit_bytes=VMEM_LIMIT)


def _sigmoid(v):
    return 1.0 / (1.0 + jnp.exp(-v))


def _silu(v):
    return v * _sigmoid(v)


def _softplus(v):
    return jnp.maximum(v, 0.0) + jnp.log1p(jnp.exp(-jnp.abs(v)))


def _layer_norm(r, g, b):
    mu = jnp.mean(r, axis=-1, keepdims=True)
    d = r - mu
    var = jnp.mean(d * d, axis=-1, keepdims=True)
    return d * lax.rsqrt(var + LN_EPS) * g + b


def _ada_body(c_ref, w_ref, b_ref, o_ref):
    o_ref[...] = jnp.dot(_silu(c_ref[...]), w_ref[...], preferred_element_type=F32,
                         precision=HIGHEST) + b_ref[...]


def _ada(c_pad, w_ada, b_ada):
    rows, d = c_pad.shape
    n = w_ada.shape[1]
    tn = 512
    return pl.pallas_call(
        _ada_body,
        grid=(n // tn,),
        in_specs=[pl.BlockSpec((rows, d), lambda j: (0, 0)),
                  pl.BlockSpec((d, tn), lambda j: (0, j)),
                  pl.BlockSpec((1, tn), lambda j: (0, j))],
        out_specs=pl.BlockSpec((rows, tn), lambda j: (0, j)),
        out_shape=jax.ShapeDtypeStruct((rows, n), F32),
        compiler_params=_cparams(("arbitrary",)),
        name="ada",
    )(c_pad, w_ada, b_ada)


def _inproj_body(x_ref, sc_ref, sh_ref, w_ref, wdt_ref, p_ref, dt_ref, h_scr):
    @pl.when(pl.program_id(1) == 0)
    def _():
        h = (x_ref[...] * (1.0 + sc_ref[0]) + sh_ref[0]).astype(BF16)
        h_scr[...] = h
        dt_ref[...] = jnp.dot(h, wdt_ref[...], preferred_element_type=F32)

    p_ref[...] = jnp.dot(h_scr[...], w_ref[...], preferred_element_type=F32)


def _inproj(x2, scale, shift, w_main, w_dt, seq):
    m, d = x2.shape
    n = w_main.shape[1]
    tm, tn = 512, 1024
    tiles_per_batch = seq // tm
    return pl.pallas_call(
        _inproj_body,
        grid=(m // tm, n // tn),
        in_specs=[pl.BlockSpec((tm, d), lambda i, j: (i, 0)),
                  pl.BlockSpec((1, 1, d), lambda i, j: (i // tiles_per_batch, 0, 0)),
                  pl.BlockSpec((1, 1, d), lambda i, j: (i // tiles_per_batch, 0, 0)),
                  pl.BlockSpec((d, tn), lambda i, j: (0, j)),
                  pl.BlockSpec((d, DT_PAD), lambda i, j: (0, 0))],
        out_specs=[pl.BlockSpec((tm, tn), lambda i, j: (i, j)),
                   pl.BlockSpec((tm, DT_PAD), lambda i, j: (i, 0))],
        out_shape=[jax.ShapeDtypeStruct((m, n), F32),
                   jax.ShapeDtypeStruct((m, DT_PAD), F32)],
        scratch_shapes=[pltpu.VMEM((tm, d), BF16)],
        compiler_params=_cparams(("arbitrary", "arbitrary")),
        name="inproj",
    )(x2, scale, shift, w_main, w_dt)


def _expand_heads(q, e3_ref):
    hi = q.astype(BF16)
    r1 = q - hi.astype(F32)
    mid = r1.astype(BF16)
    lo = (r1 - mid.astype(F32)).astype(BF16)
    pieces = jnp.concatenate([hi, mid, lo], axis=1)
    return jnp.dot(pieces, e3_ref[...], preferred_element_type=F32)


def _chunk_decay_terms(dt_ref, bias_ref, alog_ref):
    row = lax.broadcasted_iota(jnp.int32, (CHUNK, CHUNK), 0)
    col = lax.broadcasted_iota(jnp.int32, (CHUNK, CHUNK), 1)
    tril = col <= row
    triu = col >= row
    dtq = _softplus(dt_ref[...] + bias_ref[...])
    da = dtq * (-jnp.exp(alog_ref[...]))
    pre = jnp.dot(tril.astype(F32), da, preferred_element_type=F32, precision=HIGHEST)
    suf = jnp.dot(triu.astype(F32), da, preferred_element_type=F32, precision=HIGHEST)
    acum = jnp.where(col < N_HEADS, pre, suf)
    return dtq, acum, tril, triu


def _ssd_bwd_body(cur_ref, prev_ref, next_ref, dt_ref, cw_ref, cb_ref, bias_ref, alog_ref, e3_ref,
                  act_ref, yoff_ref, st_ref, *, n_chunks):
    s = pl.program_id(1)
    c = n_chunks - 1 - s

    @pl.when(s == 0)
    def _():
        st_ref[...] = jnp.zeros_like(st_ref)

    lane_tile = 512
    row = lax.broadcasted_iota(jnp.int32, (CHUNK, lane_tile), 0)
    has_prev = c > 0
    has_next = c < n_chunks - 1
    for t in range(D_XBC // lane_tile):
        sl = slice(t * lane_tile, (t + 1) * lane_tile)
        u = cur_ref[:, sl]
        pm = jnp.where(has_prev, prev_ref[:, sl], 0.0)
        nx = jnp.where(has_next, next_ref[:, sl], 0.0)
        d1 = jnp.where(row == 0, pm[HALO - 1:HALO, :], pltpu.roll(u, 1, 0))
        d2 = jnp.where(row == 0, pm[HALO - 2:HALO - 1, :],
                       jnp.where(row == 1, pm[HALO - 1:HALO, :], pltpu.roll(u, 2, 0)))
        u1 = jnp.where(row == CHUNK - 1, nx[0:1, :], pltpu.roll(u, CHUNK - 1, 0))
        u2 = jnp.where(row == CHUNK - 2, nx[0:1, :],
                       jnp.where(row == CHUNK - 1, nx[1:2, :], pltpu.roll(u, CHUNK - 2, 0)))
        w = cw_ref[:, sl]
        conv = (w[0:1, :] * d2 + w[1:2, :] * d1 + w[2:3, :] * u + w[3:4, :] * u1 + w[4:5, :] * u2
                + cb_ref[:, sl])
        act_ref[:, sl] = _silu(conv)

    dtq, acum, _, _ = _chunk_decay_terms(dt_ref, bias_ref, alog_ref)
    lane = lax.broadcasted_iota(jnp.int32, (CHUNK, DT_PAD), 1)
    acum = jnp.where((lane >= N_HEADS) & (lane < 2 * N_HEADS), acum, 0.0)
    total = acum[0:1, :]
    eb_x = _expand_heads(jnp.exp(acum), e3_ref)
    wb_x = _expand_heads(dtq * jnp.exp(total - acum), e3_ref)
    dec_x = eb_x[0:1, :]

    for g in range(N_GROUPS):
        ch = slice(g * GROUP_W, (g + 1) * GROUP_W)
        b_g = act_ref[:, D_SSM + g * D_STATE:D_SSM + (g + 1) * D_STATE].astype(BF16)
        c_g = act_ref[:, D_SSM + N_GROUPS * D_STATE + g * D_STATE:
                      D_SSM + N_GROUPS * D_STATE + (g + 1) * D_STATE].astype(BF16)
        st = st_ref[g]
        yoff_ref[:, ch] = jnp.dot(c_g, st.astype(BF16), preferred_element_type=F32) * eb_x[:, ch]
        xw = (act_ref[:, ch] * wb_x[:, ch]).astype(BF16)
        upd = lax.dot_general(b_g, xw, (((0,), (0,)), ((), ())), preferred_element_type=F32)
        st_ref[g] = st * dec_x[:, ch] + upd


def _ssd_bwd(p_main, dt, conv_w, conv_b, bias_q, alog_q, e3b, batch, seq):
    m = p_main.shape[0]
    nc = seq // CHUNK
    rb = CHUNK // HALO
    last_halo = m // HALO - 1

    def cur_map(b, s):
        return (b * nc + (nc - 1 - s), 0)

    def prev_map(b, s):
        return (jnp.maximum((b * nc + (nc - 1 - s)) * rb - 1, 0), 0)

    def next_map(b, s):
        return (jnp.minimum((b * nc + (nc - 1 - s) + 1) * rb, last_halo), 0)

    const = lambda b, s: (0, 0)
    return pl.pallas_call(
        functools.partial(_ssd_bwd_body, n_chunks=nc),
        grid=(batch, nc),
        in_specs=[pl.BlockSpec((CHUNK, D_XBC), cur_map),
                  pl.BlockSpec((HALO, D_XBC), prev_map),
                  pl.BlockSpec((HALO, D_XBC), next_map),
                  pl.BlockSpec((CHUNK, DT_PAD), cur_map),
                  pl.BlockSpec((HALO, D_XBC), const),
                  pl.BlockSpec((1, D_XBC), const),
                  pl.BlockSpec((1, DT_PAD), const),
                  pl.BlockSpec((1, DT_PAD), const),
                  pl.BlockSpec((3 * DT_PAD, D_SSM), const)],
        out_specs=[pl.BlockSpec((CHUNK, D_XBC), cur_map),
                   pl.BlockSpec((CHUNK, D_SSM), cur_map)],
        out_shape=[jax.ShapeDtypeStruct((m, D_XBC), F32),
                   jax.ShapeDtypeStruct((m, D_SSM), F32)],
        scratch_shapes=[pltpu.VMEM((N_GROUPS, D_STATE, GROUP_W), F32)],
        compiler_params=_cparams(("arbitrary", "arbitrary")),
        name="ssd_bwd",
    )(p_main, p_main, p_main, dt, conv_w, conv_b, bias_q, alog_q, e3b)


def _ssd_fwd_body(act_ref, z_ref, dt_ref, yoffb_ref, bias_ref, alog_ref, dskip_ref, normw_ref, e3_ref,
                  y_ref, st_ref):
    @pl.when(pl.program_id(1) == 0)
    def _():
        st_ref[...] = jnp.zeros_like(st_ref)

    dtq, acum, tril, triu = _chunk_decay_terms(dt_ref, bias_ref, alog_ref)
    acum_t = acum.T
    dtq_t = dtq.T
    lane = lax.broadcasted_iota(jnp.int32, (CHUNK, DT_PAD), 1)
    acum_f = jnp.where(lane < N_HEADS, acum, 0.0)
    total = acum_f[CHUNK - 1:CHUNK, :]
    ef_x = _expand_heads(jnp.exp(acum_f), e3_ref)
    wf_x = _expand_heads(dtq * jnp.exp(total - acum_f), e3_ref)
    dec_x = ef_x[CHUNK - 1:CHUNK, :]
    head_of_lane = lax.broadcasted_iota(jnp.int32, (CHUNK, GROUP_W), 1) // HEAD_DIM
    neg_inf = jnp.float32(-jnp.inf)

    for g in range(N_GROUPS):
        ch = slice(g * GROUP_W, (g + 1) * GROUP_W)
        b_g = act_ref[:, D_SSM + g * D_STATE:D_SSM + (g + 1) * D_STATE].astype(BF16)
        c_g = act_ref[:, D_SSM + N_GROUPS * D_STATE + g * D_STATE:
                      D_SSM + N_GROUPS * D_STATE + (g + 1) * D_STATE].astype(BF16)
        x_g = act_ref[:, ch]
        scores = lax.dot_general(c_g, b_g, (((1,), (1,)), ((), ())), preferred_element_type=F32)

        mats = []
        blocks = []
        for r in range(HEADS_PER_GROUP):
            hf = g * HEADS_PER_GROUP + r
            hb = N_HEADS + hf
            seg_f = acum[:, hf:hf + 1] - acum_t[hf:hf + 1, :]
            seg_b = acum[:, hb:hb + 1] - acum_t[hb:hb + 1, :]
            lf = jnp.exp(jnp.where(tril, seg_f, neg_inf)) * dtq_t[hf:hf + 1, :]
            ub = jnp.exp(jnp.where(triu, seg_b, neg_inf)) * dtq_t[hb:hb + 1, :]
            mats.append((scores * (lf + ub)).astype(BF16))
            blocks.append(jnp.where(head_of_lane == r, x_g, 0.0).astype(BF16))
        y_diag = jnp.dot(jnp.concatenate(mats, axis=1), jnp.concatenate(blocks, axis=0),
                         preferred_element_type=F32)

        st = st_ref[g]
        y_off = jnp.dot(c_g, st.astype(BF16), preferred_element_type=F32) * ef_x[:, ch]
        y = y_diag + y_off + yoffb_ref[:, ch] + dskip_ref[:, ch] * x_g

        xw = (x_g * wf_x[:, ch]).astype(BF16)
        upd = lax.dot_general(b_g, xw, (((0,), (0,)), ((), ())), preferred_element_type=F32)
        st_ref[g] = st * dec_x[:, ch] + upd

        y = y * _silu(z_ref[:, ch])
        ms = jnp.mean(y * y, axis=-1, keepdims=True)
        y_ref[:, ch] = (y * lax.rsqrt(ms + RMS_EPS) * normw_ref[:, ch]).astype(BF16)


def _ssd_fwd(act, p_main, dt, yoffb, bias_q, alog_q, dskip_x, norm_w, e3f, batch, seq):
    m = act.shape[0]
    nc = seq // CHUNK
    cur = lambda b, s: (b * nc + s, 0)
    const = lambda b, s: (0, 0)
    return pl.pallas_call(
        _ssd_fwd_body,
        grid=(batch, nc),
        in_specs=[pl.BlockSpec((CHUNK, D_XBC), cur),
                  pl.BlockSpec((CHUNK, D_SSM), lambda b, s: (b * nc + s, COL_Z)),
                  pl.BlockSpec((CHUNK, DT_PAD), cur),
                  pl.BlockSpec((CHUNK, D_SSM), cur),
                  pl.BlockSpec((1, DT_PAD), const),
                  pl.BlockSpec((1, DT_PAD), const),
                  pl.BlockSpec((1, D_SSM), const),
                  pl.BlockSpec((1, D_SSM), const),
                  pl.BlockSpec((3 * DT_PAD, D_SSM), const)],
        out_specs=pl.BlockSpec((CHUNK, D_SSM), cur),
        out_shape=jax.ShapeDtypeStruct((m, D_SSM), BF16),
        scratch_shapes=[pltpu.VMEM((N_GROUPS, D_STATE, GROUP_W), F32)],
        compiler_params=_cparams(("arbitrary", "arbitrary")),
        name="ssd_fwd",
    )(act, p_main, dt, yoffb, bias_q, alog_q, dskip_x, norm_w, e3f)


def _shortconv_body(h_ref, b_ref, c_ref, hp_ref, cp_ref, hn_ref, cn_ref, cw_ref, nw_ref, y_ref,
                    *, tiles_per_seq, rows):
    i = pl.program_id(0)
    t = i % tiles_per_seq
    has_prev = t > 0
    has_next = t < tiles_per_seq - 1
    lane_tile = 512
    row = lax.broadcasted_iota(jnp.int32, (rows, lane_tile), 0)
    group = D_SC // SC_GROUPS
    for q in range(D_SC // lane_tile):
        sl = slice(q * lane_tile, (q + 1) * lane_tile)
        v = c_ref[:, sl] * h_ref[:, sl]
        vp = jnp.where(has_prev, cp_ref[HALO - 1:HALO, sl] * hp_ref[HALO - 1:HALO, sl], 0.0)
        vn = jnp.where(has_next, cn_ref[0:1, sl] * hn_ref[0:1, sl], 0.0)
        down = jnp.where(row == 0, vp, pltpu.roll(v, 1, 0))
        up = jnp.where(row == rows - 1, vn, pltpu.roll(v, rows - 1, 0))
        w = cw_ref[:, sl]
        y = b_ref[:, sl] * (w[0:1, :] * down + w[1:2, :] * v + w[2:3, :] * up)
        for k in range(lane_tile // group):
            gs = slice(k * group, (k + 1) * group)
            yg = y[:, gs]
            ms = jnp.mean(yg * yg, axis=-1, keepdims=True)
            lo = q * lane_tile + k * group
            y_ref[:, lo:lo + group] = (yg * lax.rsqrt(ms + RMS_EPS) * nw_ref[:, lo:lo + group]).astype(BF16)


def _shortconv(p_main, conv_w, norm_w, seq):
    m = p_main.shape[0]
    rows = 256
    tiles_per_seq = seq // rows
    rb = rows // HALO
    last_halo = m // HALO - 1

    def cur(col):
        return lambda i: (i, col)

    def prev(col):
        return lambda i: (jnp.maximum(i * rb - 1, 0), col)

    def nxt(col):
        return lambda i: (jnp.minimum((i + 1) * rb, last_halo), col)

    const = lambda i: (0, 0)
    return pl.pallas_call(
        functools.partial(_shortconv_body, tiles_per_seq=tiles_per_seq, rows=rows),
        grid=(m // rows,),
        in_specs=[pl.BlockSpec((rows, D_SC), cur(COL_H)),
                  pl.BlockSpec((rows, D_SC), cur(COL_B)),
                  pl.BlockSpec((rows, D_SC), cur(COL_C)),
                  pl.BlockSpec((HALO, D_SC), prev(COL_H)),
                  pl.BlockSpec((HALO, D_SC), prev(COL_C)),
                  pl.BlockSpec((HALO, D_SC), nxt(COL_H)),
                  pl.BlockSpec((HALO, D_SC), nxt(COL_C)),
                  pl.BlockSpec((HALO, D_SC), const),
                  pl.BlockSpec((1, D_SC), const)],
        out_specs=pl.BlockSpec((rows, D_SC), lambda i: (i, 0)),
        out_shape=jax.ShapeDtypeStruct((m, D_SC), BF16),
        compiler_params=_cparams(("arbitrary",)),
        name="shortconv",
    )(p_main, p_main, p_main, p_main, p_main, p_main, p_main, conv_w, norm_w)


def _residual_layer_norm(o_ref, x_ref, gate_ref, lng_ref, lnb_ref, alpha, modulated=None):
    rows = o_ref.shape[0]
    for r0 in range(0, rows, LN_ROWS):
        sl = slice(r0, r0 + LN_ROWS)
        r = alpha * x_ref[sl, :] + (1.0 + gate_ref[0]) * o_ref[sl, :]
        y = _layer_norm(r, lng_ref[...], lnb_ref[...])
        o_ref[sl, :] = y
        if modulated is not None:
            h_ref, sc_ref, sh_ref = modulated
            h_ref[sl, :] = (y * (1.0 + sc_ref[0]) + sh_ref[0]).astype(BF16)


def _outproj_body(ys_ref, yc_ref, w_ref, x_ref, gate_ref, sc_ref, sh_ref, lng_ref, lnb_ref, x1_ref, h2_ref,
                  *, alpha, k_half):
    k = pl.program_id(1)

    @pl.when(k == 0)
    def _():
        x1_ref[...] = jnp.zeros_like(x1_ref)

    def accumulate(lhs_ref):
        lhs = lhs_ref[...]
        for n0 in range(0, x1_ref.shape[1], ACC_COLS):
            cols = slice(n0, n0 + ACC_COLS)
            x1_ref[:, cols] += jnp.dot(lhs, w_ref[:, cols], preferred_element_type=F32)

    @pl.when(k < k_half)
    def _():
        accumulate(ys_ref)

    @pl.when(k >= k_half)
    def _():
        accumulate(yc_ref)

    @pl.when(k == 2 * k_half - 1)
    def _():
        _residual_layer_norm(x1_ref, x_ref, gate_ref, lng_ref, lnb_ref, alpha,
                             modulated=(h2_ref, sc_ref, sh_ref))


def _outproj(y_ssm, y_sc, w_out, x2, gate, scale2, shift2, ln_g, ln_b, seq, alpha):
    m, d = x2.shape
    tm, tk = 512, 512
    k_half = D_SSM // tk
    tiles_per_batch = seq // tm
    mod_map = lambda i, k: (i // tiles_per_batch, 0, 0)
    const = lambda i, k: (0, 0)
    row_tile = lambda i, k: (i, 0)
    return pl.pallas_call(
        functools.partial(_outproj_body, alpha=alpha, k_half=k_half),
        grid=(m // tm, 2 * k_half),
        in_specs=[pl.BlockSpec((tm, tk), lambda i, k: (i, jnp.minimum(k, k_half - 1))),
                  pl.BlockSpec((tm, tk), lambda i, k: (i, jnp.maximum(k - k_half, 0))),
                  pl.BlockSpec((tk, d), lambda i, k: (k, 0)),
                  pl.BlockSpec((tm, d), row_tile, pipeline_mode=pl.Buffered(1)),
                  pl.BlockSpec((1, 1, d), mod_map),
                  pl.BlockSpec((1, 1, d), mod_map),
                  pl.BlockSpec((1, 1, d), mod_map),
                  pl.BlockSpec((1, d), const),
                  pl.BlockSpec((1, d), const)],
        out_specs=[pl.BlockSpec((tm, d), row_tile),
                   pl.BlockSpec((tm, d), row_tile)],
        out_shape=[jax.ShapeDtypeStruct((m, d), F32),
                   jax.ShapeDtypeStruct((m, d), BF16)],
        compiler_params=_cparams(("arbitrary", "arbitrary")),
        name="outproj",
    )(y_ssm, y_sc, w_out, x2, gate, scale2, shift2, ln_g, ln_b)


def _ffn_body(h2_ref, x1_ref, gate_ref, wu_ref, wd_ref, lng_ref, lnb_ref, o_ref, *, alpha):
    f = pl.program_id(1)

    @pl.when(f == 0)
    def _():
        o_ref[...] = jnp.zeros_like(o_ref)

    u = jnp.maximum(jnp.dot(h2_ref[...], wu_ref[...], preferred_element_type=F32), 0.0)
    u = (u * u).astype(BF16)
    for n0 in range(0, o_ref.shape[1], ACC_COLS):
        cols = slice(n0, n0 + ACC_COLS)
        o_ref[:, cols] += jnp.dot(u, wd_ref[:, cols], preferred_element_type=F32)

    @pl.when(f == pl.num_programs(1) - 1)
    def _():
        _residual_layer_norm(o_ref, x1_ref, gate_ref, lng_ref, lnb_ref, alpha)


def _ffn(h2, x1, gate, w_up, w_down, ln_g, ln_b, seq, alpha):
    m, d = x1.shape
    ff = w_up.shape[1]
    tm, tf = 512, 512
    tiles_per_batch = seq // tm
    mod_map = lambda i, f: (i // tiles_per_batch, 0, 0)
    const = lambda i, f: (0, 0)
    row_tile = lambda i, f: (i, 0)
    return pl.pallas_call(
        functools.partial(_ffn_body, alpha=alpha),
        grid=(m // tm, ff // tf),
        in_specs=[pl.BlockSpec((tm, d), row_tile),
                  pl.BlockSpec((tm, d), row_tile, pipeline_mode=pl.Buffered(1)),
                  pl.BlockSpec((1, 1, d), mod_map),
                  pl.BlockSpec((d, tf), lambda i, f: (0, f)),
                  pl.BlockSpec((tf, d), lambda i, f: (f, 0)),
                  pl.BlockSpec((1, d), const),
                  pl.BlockSpec((1, d), const)],
        out_specs=pl.BlockSpec((tm, d), row_tile),
        out_shape=jax.ShapeDtypeStruct((m, d), F32),
        compiler_params=_cparams(("arbitrary", "arbitrary")),
        name="ffn",
    )(h2, x1, gate, w_up, w_down, ln_g, ln_b)


def _head_expansion(lane_offset):
    rows = jnp.arange(DT_PAD)[:, None]
    heads = jnp.arange(D_SSM)[None, :] // HEAD_DIM
    e = (rows == heads + lane_offset).astype(BF16)
    return jnp.tile(e, (3, 1))


def _pad_rows(a, rows):
    return jnp.pad(a, ((0, rows - a.shape[0]), (0, 0)))


def kernel(x, c, w_ada, b_ada, w_in, ssm_conv_w, ssm_conv_b, ssm_dt_bias_f, ssm_dt_bias_b, ssm_a_log_f, ssm_a_log_b, ssm_d, ssm_norm_w, sc_conv_w, sc_norm_w, w_out, ln1_g, ln1_b, w_up, w_down, ln2_g, ln2_b):
    batch, seq, d = x.shape
    depth = w_ada.shape[0]
    alpha = float((2 * depth) ** 0.25)
    m = batch * seq
    dt_lo = D_SSM + D_XBC
    dt_hi = dt_lo + 2 * N_HEADS
    e3f = _head_expansion(0)
    e3b = _head_expansion(N_HEADS)
    c_pad = _pad_rows(c, HALO)
    x2 = x.reshape(m, d)

    for l in range(depth):
        mod = _ada(c_pad, w_ada[l], b_ada[l][None, :])[:batch]
        shift1, scale1, gate1, shift2, scale2, gate2 = [
            t[:, None, :] for t in jnp.split(mod, N_MOD, axis=-1)]

        w = w_in[l]
        w_main = jnp.concatenate([w[:, D_SSM:dt_lo], w[:, :D_SSM], w[:, dt_hi:]], axis=1).astype(BF16)
        w_dt = jnp.pad(w[:, dt_lo:dt_hi], ((0, 0), (0, DT_PAD - 2 * N_HEADS))).astype(BF16)
        p_main, dt = _inproj(x2, scale1, shift1, w_main, w_dt, seq)

        zeros_q = jnp.zeros((DT_PAD - 2 * N_HEADS,), F32)
        bias_q = jnp.concatenate([ssm_dt_bias_f[l], ssm_dt_bias_b[l], zeros_q])[None, :]
        alog_q = jnp.concatenate([ssm_a_log_f[l], ssm_a_log_b[l], zeros_q])[None, :]
        act, yoffb = _ssd_bwd(p_main, dt, _pad_rows(ssm_conv_w[l], HALO), ssm_conv_b[l][None, :],
                              bias_q, alog_q, e3b, batch, seq)
        dskip_x = jnp.repeat(ssm_d[l], HEAD_DIM)[None, :]
        y_ssm = _ssd_fwd(act, p_main, dt, yoffb, bias_q, alog_q, dskip_x, ssm_norm_w[l][None, :], e3f,
                         batch, seq)
        y_sc = _shortconv(p_main, _pad_rows(sc_conv_w[l], HALO), sc_norm_w[l][None, :], seq)

        x1, h2 = _outproj(y_ssm, y_sc, w_out[l].astype(BF16), x2, gate1, scale2, shift2,
                          ln1_g[l][None, :], ln1_b[l][None, :], seq, alpha)
        x2 = _ffn(h2, x1, gate2, w_up[l].astype(BF16), w_down[l].astype(BF16),
                  ln2_g[l][None, :], ln2_b[l][None, :], seq, alpha)
    return x2.reshape(batch, seq, d)
```

```python
import functools

import jax
import jax.numpy as jnp
from jax import lax
from jax.experimental import pallas as pl
from jax.experimental.pallas import tpu as pltpu

F32 = jnp.float32
BF16 = jnp.bfloat16
HIGHEST = lax.Precision.HIGHEST

D_MODEL = 4096
D_SSM = 2048
D_SC = 2048
HEAD_DIM = 64
N_HEADS = 32
N_GROUPS = 8
HEADS_PER_GROUP = 4
D_STATE = 128
CHUNK = 128
SSM_CONV = 5
SC_CONV = 3
SC_GROUPS = 16
D_XBC = D_SSM + 2 * N_GROUPS * D_STATE
D_FF = 4 * D_MODEL
N_MOD = 6
LN_EPS = 1e-5
RMS_EPS = 1e-5
GROUP_W = HEADS_PER_GROUP * HEAD_DIM
D_MAIN = D_XBC + D_SSM + 3 * D_SC
DT_PAD = 128
HALO = 8
LN_ROWS = 64
ACC_COLS = 512

COL_Z = 2
COL_H = 3
COL_B = 4
COL_C = 5

VMEM_LIMIT = 60 * 1024 * 1024


def _cparams(sem):
    return pltpu.CompilerParams(dimension_semantics=sem, vmem_limit_bytes=VMEM_LIMIT)


def _sigmoid(v):
    return 1.0 / (1.0 + jnp.exp(-v))


def _silu(v):
    return v * _sigmoid(v)


def _softplus(v):
    return jnp.maximum(v, 0.0) + jnp.log1p(jnp.exp(-jnp.abs(v)))


def _layer_norm(r, g, b):
    mu = jnp.mean(r, axis=-1, keepdims=True)
    d = r - mu
    var = jnp.mean(d * d, axis=-1, keepdims=True)
    return d * lax.rsqrt(var + LN_EPS) * g + b


def _ada_body(c_ref, w_ref, b_ref, o_ref):
    o_ref[...] = jnp.dot(_silu(c_ref[...]), w_ref[...], preferred_element_type=F32,
                         precision=HIGHEST) + b_ref[...]


def _ada(c_pad, w_ada, b_ada):
    rows, d = c_pad.shape
    n = w_ada.shape[1]
    tn = 512
    return pl.pallas_call(
        _ada_body,
        grid=(n // tn,),
        in_specs=[pl.BlockSpec((rows, d), lambda j: (0, 0)),
                  pl.BlockSpec((d, tn), lambda j: (0, j)),
                  pl.BlockSpec((1, tn), lambda j: (0, j))],
        out_specs=pl.BlockSpec((rows, tn), lambda j: (0, j)),
        out_shape=jax.ShapeDtypeStruct((rows, n), F32),
        compiler_params=_cparams(("arbitrary",)),
        name="ada",
    )(c_pad, w_ada, b_ada)


def _inproj_body(x_ref, sc_ref, sh_ref, w_ref, wdt_ref, p_ref, dt_ref, h_scr):
    @pl.when(pl.program_id(1) == 0)
    def _():
        h = (x_ref[...] * (1.0 + sc_ref[0]) + sh_ref[0]).astype(BF16)
        h_scr[...] = h
        dt_ref[...] = jnp.dot(h, wdt_ref[...], preferred_element_type=F32)

    p_ref[...] = jnp.dot(h_scr[...], w_ref[...], preferred_element_type=F32)


def _inproj(x2, scale, shift, w_main, w_dt, seq):
    m, d = x2.shape
    n = w_main.shape[1]
    tm, tn = 1024, 512
    tiles_per_batch = seq // tm
    return pl.pallas_call(
        _inproj_body,
        grid=(m // tm, n // tn),
        in_specs=[pl.BlockSpec((tm, d), lambda i, j: (i, 0)),
                  pl.BlockSpec((1, 1, d), lambda i, j: (i // tiles_per_batch, 0, 0)),
                  pl.BlockSpec((1, 1, d), lambda i, j: (i // tiles_per_batch, 0, 0)),
                  pl.BlockSpec((d, tn), lambda i, j: (0, j)),
                  pl.BlockSpec((d, DT_PAD), lambda i, j: (0, 0))],
        out_specs=[pl.BlockSpec((tm, tn), lambda i, j: (i, j)),
                   pl.BlockSpec((tm, DT_PAD), lambda i, j: (i, 0))],
        out_shape=[jax.ShapeDtypeStruct((m, n), F32),
                   jax.ShapeDtypeStruct((m, DT_PAD), F32)],
        scratch_shapes=[pltpu.VMEM((tm, d), BF16)],
        compiler_params=_cparams(("arbitrary", "arbitrary")),
        name="inproj",
    )(x2, scale, shift, w_main, w_dt)


def _expand_heads(q, e3_ref):
    hi = q.astype(BF16)
    r1 = q - hi.astype(F32)
    mid = r1.astype(BF16)
    lo = (r1 - mid.astype(F32)).astype(BF16)
    pieces = jnp.concatenate([hi, mid, lo], axis=1)
    return jnp.dot(pieces, e3_ref[...], preferred_element_type=F32)


def _chunk_decay_terms(dtq, alog_ref):
    row = lax.broadcasted_iota(jnp.int32, (CHUNK, CHUNK), 0)
    col = lax.broadcasted_iota(jnp.int32, (CHUNK, CHUNK), 1)
    tril = col <= row
    triu = col >= row
    da = dtq * (-jnp.exp(alog_ref[...]))
    pre = jnp.dot(tril.astype(F32), da, preferred_element_type=F32, precision=HIGHEST)
    suf = jnp.dot(triu.astype(F32), da, preferred_element_type=F32, precision=HIGHEST)
    acum = jnp.where(col < N_HEADS, pre, suf)
    return acum, tril, triu


def _ssd_bwd_body(cur_ref, prev_ref, next_ref, dt_ref, cw_ref, cb_ref, bias_ref, alog_ref, e3_ref,
                  act_ref, yoff_ref, dtq_ref, st_ref, *, n_chunks):
    s = pl.program_id(1)
    c = n_chunks - 1 - s

    @pl.when(s == 0)
    def _():
        st_ref[...] = jnp.zeros_like(st_ref)

    lane_tile = 512
    rows = CHUNK + 2 * HALO
    has_prev = c > 0
    has_next = c < n_chunks - 1
    for t in range(D_XBC // lane_tile):
        sl = slice(t * lane_tile, (t + 1) * lane_tile)
        stack = jnp.concatenate([jnp.where(has_prev, prev_ref[:, sl], 0.0), cur_ref[:, sl],
                                 jnp.where(has_next, next_ref[:, sl], 0.0)], axis=0)
        w = cw_ref[:, sl]
        conv = cb_ref[:, sl] + w[SSM_CONV // 2:SSM_CONV // 2 + 1, :] * stack[HALO:HALO + CHUNK, :]
        for k in range(SSM_CONV):
            shift = SSM_CONV // 2 - k
            if shift != 0:
                tap = pltpu.roll(stack, shift % rows, 0)[HALO:HALO + CHUNK, :]
                conv = conv + w[k:k + 1, :] * tap
        act_ref[:, sl] = _silu(conv)

    dtq = _softplus(dt_ref[...] + bias_ref[...])
    dtq_ref[...] = dtq
    acum, _, _ = _chunk_decay_terms(dtq, alog_ref)
    lane = lax.broadcasted_iota(jnp.int32, (CHUNK, DT_PAD), 1)
    acum = jnp.where((lane >= N_HEADS) & (lane < 2 * N_HEADS), acum, 0.0)
    total = acum[0:1, :]
    eb_x = _expand_heads(jnp.exp(acum), e3_ref)
    wb_x = _expand_heads(dtq * jnp.exp(total - acum), e3_ref)
    dec_x = eb_x[0:1, :]

    for g in range(N_GROUPS):
        ch = slice(g * GROUP_W, (g + 1) * GROUP_W)
        b_g = act_ref[:, D_SSM + g * D_STATE:D_SSM + (g + 1) * D_STATE].astype(BF16)
        c_g = act_ref[:, D_SSM + N_GROUPS * D_STATE + g * D_STATE:
                      D_SSM + N_GROUPS * D_STATE + (g + 1) * D_STATE].astype(BF16)
        st = st_ref[g]
        yoff_ref[:, ch] = jnp.dot(c_g, st.astype(BF16), preferred_element_type=F32) * eb_x[:, ch]
        xw = (act_ref[:, ch] * wb_x[:, ch]).astype(BF16)
        upd = lax.dot_general(b_g, xw, (((0,), (0,)), ((), ())), preferred_element_type=F32)
        st_ref[g] = st * dec_x[:, ch] + upd


def _ssd_bwd(p_main, dt, conv_w, conv_b, bias_q, alog_q, e3b, batch, seq):
    m = p_main.shape[0]
    nc = seq // CHUNK
    rb = CHUNK // HALO
    last_halo = m // HALO - 1

    def cur_map(b, s):
        return (b * nc + (nc - 1 - s), 0)

    def prev_map(b, s):
        return (jnp.maximum((b * nc + (nc - 1 - s)) * rb - 1, 0), 0)

    def next_map(b, s):
        return (jnp.minimum((b * nc + (nc - 1 - s) + 1) * rb, last_halo), 0)

    const = lambda b, s: (0, 0)
    return pl.pallas_call(
        functools.partial(_ssd_bwd_body, n_chunks=nc),
        grid=(batch, nc),
        in_specs=[pl.BlockSpec((CHUNK, D_XBC), cur_map),
                  pl.BlockSpec((HALO, D_XBC), prev_map),
                  pl.BlockSpec((HALO, D_XBC), next_map),
                  pl.BlockSpec((CHUNK, DT_PAD), cur_map),
                  pl.BlockSpec((HALO, D_XBC), const),
                  pl.BlockSpec((1, D_XBC), const),
                  pl.BlockSpec((1, DT_PAD), const),
                  pl.BlockSpec((1, DT_PAD), const),
                  pl.BlockSpec((3 * DT_PAD, D_SSM), const)],
        out_specs=[pl.BlockSpec((CHUNK, D_XBC), cur_map),
                   pl.BlockSpec((CHUNK, D_SSM), cur_map),
                   pl.BlockSpec((CHUNK, DT_PAD), cur_map)],
        out_shape=[jax.ShapeDtypeStruct((m, D_XBC), F32),
                   jax.ShapeDtypeStruct((m, D_SSM), F32),
                   jax.ShapeDtypeStruct((m, DT_PAD), F32)],
        scratch_shapes=[pltpu.VMEM((N_GROUPS, D_STATE, GROUP_W), F32)],
        compiler_params=_cparams(("arbitrary", "arbitrary")),
        name="ssd_bwd",
    )(p_main, p_main, p_main, dt, conv_w, conv_b, bias_q, alog_q, e3b)


def _ssd_fwd_body(act_ref, z_ref, dtq_ref, yoffb_ref, alog_ref, dskip_ref, normw_ref, e3_ref,
                  y_ref, st_ref):
    @pl.when(pl.program_id(1) == 0)
    def _():
        st_ref[...] = jnp.zeros_like(st_ref)

    dtq = dtq_ref[...]
    acum, tril, triu = _chunk_decay_terms(dtq, alog_ref)
    acum_t = acum.T
    dtq_t = dtq.T
    lane = lax.broadcasted_iota(jnp.int32, (CHUNK, DT_PAD), 1)
    acum_f = jnp.where(lane < N_HEADS, acum, 0.0)
    total = acum_f[CHUNK - 1:CHUNK, :]
    ef_x = _expand_heads(jnp.exp(acum_f), e3_ref)
    wf_x = _expand_heads(dtq * jnp.exp(total - acum_f), e3_ref)
    dec_x = ef_x[CHUNK - 1:CHUNK, :]
    head_of_lane = lax.broadcasted_iota(jnp.int32, (CHUNK, GROUP_W), 1) // HEAD_DIM
    neg_inf = jnp.float32(-jnp.inf)

    for g in range(N_GROUPS):
        ch = slice(g * GROUP_W, (g + 1) * GROUP_W)
        b_g = act_ref[:, D_SSM + g * D_STATE:D_SSM + (g + 1) * D_STATE].astype(BF16)
        c_g = act_ref[:, D_SSM + N_GROUPS * D_STATE + g * D_STATE:
                      D_SSM + N_GROUPS * D_STATE + (g + 1) * D_STATE].astype(BF16)
        x_g = act_ref[:, ch]
        scores = lax.dot_general(c_g, b_g, (((1,), (1,)), ((), ())), preferred_element_type=F32)

        mats = []
        blocks = []
        for r in range(HEADS_PER_GROUP):
            hf = g * HEADS_PER_GROUP + r
            hb = N_HEADS + hf
            seg_f = acum[:, hf:hf + 1] - acum_t[hf:hf + 1, :]
            seg_b = acum[:, hb:hb + 1] - acum_t[hb:hb + 1, :]
            lf = jnp.exp(jnp.where(tril, seg_f, neg_inf)) * dtq_t[hf:hf + 1, :]
            ub = jnp.exp(jnp.where(triu, seg_b, neg_inf)) * dtq_t[hb:hb + 1, :]
            mats.append((scores * (lf + ub)).astype(BF16))
            blocks.append(jnp.where(head_of_lane == r, x_g, 0.0).astype(BF16))
        y_diag = jnp.dot(jnp.concatenate(mats, axis=1), jnp.concatenate(blocks, axis=0),
                         preferred_element_type=F32)

        st = st_ref[g]
        y_off = jnp.dot(c_g, st.astype(BF16), preferred_element_type=F32) * ef_x[:, ch]
        y = y_diag + y_off + yoffb_ref[:, ch] + dskip_ref[:, ch] * x_g

        xw = (x_g * wf_x[:, ch]).astype(BF16)
        upd = lax.dot_general(b_g, xw, (((0,), (0,)), ((), ())), preferred_element_type=F32)
        st_ref[g] = st * dec_x[:, ch] + upd

        y = y * _silu(z_ref[:, ch])
        ms = jnp.mean(y * y, axis=-1, keepdims=True)
        y_ref[:, ch] = (y * lax.rsqrt(ms + RMS_EPS) * normw_ref[:, ch]).astype(BF16)


def _ssd_fwd(act, p_main, dtq, yoffb, alog_q, dskip_x, norm_w, e3f, batch, seq):
    m = act.shape[0]
    nc = seq // CHUNK
    cur = lambda b, s: (b * nc + s, 0)
    const = lambda b, s: (0, 0)
    return pl.pallas_call(
        _ssd_fwd_body,
        grid=(batch, nc),
        in_specs=[pl.BlockSpec((CHUNK, D_XBC), cur),
                  pl.BlockSpec((CHUNK, D_SSM), lambda b, s: (b * nc + s, COL_Z)),
                  pl.BlockSpec((CHUNK, DT_PAD), cur),
                  pl.BlockSpec((CHUNK, D_SSM), cur),
                  pl.BlockSpec((1, DT_PAD), const),
                  pl.BlockSpec((1, D_SSM), const),
                  pl.BlockSpec((1, D_SSM), const),
                  pl.BlockSpec((3 * DT_PAD, D_SSM), const)],
        out_specs=pl.BlockSpec((CHUNK, D_SSM), cur),
        out_shape=jax.ShapeDtypeStruct((m, D_SSM), BF16),
        scratch_shapes=[pltpu.VMEM((N_GROUPS, D_STATE, GROUP_W), F32)],
        compiler_params=_cparams(("arbitrary", "arbitrary")),
        name="ssd_fwd",
    )(act, p_main, dtq, yoffb, alog_q, dskip_x, norm_w, e3f)


def _shortconv_body(h_ref, b_ref, c_ref, hp_ref, cp_ref, hn_ref, cn_ref, cw_ref, nw_ref, y_ref,
                    *, tiles_per_seq, rows):
    i = pl.program_id(0)
    t = i % tiles_per_seq
    has_prev = t > 0
    has_next = t < tiles_per_seq - 1
    lane_tile = 512
    row = lax.broadcasted_iota(jnp.int32, (rows, lane_tile), 0)
    group = D_SC // SC_GROUPS
    for q in range(D_SC // lane_tile):
        sl = slice(q * lane_tile, (q + 1) * lane_tile)
        v = c_ref[:, sl] * h_ref[:, sl]
        vp = jnp.where(has_prev, cp_ref[HALO - 1:HALO, sl] * hp_ref[HALO - 1:HALO, sl], 0.0)
        vn = jnp.where(has_next, cn_ref[0:1, sl] * hn_ref[0:1, sl], 0.0)
        down = jnp.where(row == 0, vp, pltpu.roll(v, 1, 0))
        up = jnp.where(row == rows - 1, vn, pltpu.roll(v, rows - 1, 0))
        w = cw_ref[:, sl]
        y = b_ref[:, sl] * (w[0:1, :] * down + w[1:2, :] * v + w[2:3, :] * up)
        for k in range(lane_tile // group):
            gs = slice(k * group, (k + 1) * group)
            yg = y[:, gs]
            ms = jnp.mean(yg * yg, axis=-1, keepdims=True)
            lo = q * lane_tile + k * group
            y_ref[:, lo:lo + group] = (yg * lax.rsqrt(ms + RMS_EPS) * nw_ref[:, lo:lo + group]).astype(BF16)


def _shortconv(p_main, conv_w, norm_w, seq):
    m = p_main.shape[0]
    rows = 256
    tiles_per_seq = seq // rows
    rb = rows // HALO
    last_halo = m // HALO - 1

    def cur(col):
        return lambda i: (i, col)

    def prev(col):
        return lambda i: (jnp.maximum(i * rb - 1, 0), col)

    def nxt(col):
        return lambda i: (jnp.minimum((i + 1) * rb, last_halo), col)

    const = lambda i: (0, 0)
    return pl.pallas_call(
        functools.partial(_shortconv_body, tiles_per_seq=tiles_per_seq, rows=rows),
        grid=(m // rows,),
        in_specs=[pl.BlockSpec((rows, D_SC), cur(COL_H)),
                  pl.BlockSpec((rows, D_SC), cur(COL_B)),
                  pl.BlockSpec((rows, D_SC), cur(COL_C)),
                  pl.BlockSpec((HALO, D_SC), prev(COL_H)),
                  pl.BlockSpec((HALO, D_SC), prev(COL_C)),
                  pl.BlockSpec((HALO, D_SC), nxt(COL_H)),
                  pl.BlockSpec((HALO, D_SC), nxt(COL_C)),
                  pl.BlockSpec((HALO, D_SC), const),
                  pl.BlockSpec((1, D_SC), const)],
        out_specs=pl.BlockSpec((rows, D_SC), lambda i: (i, 0)),
        out_shape=jax.ShapeDtypeStruct((m, D_SC), BF16),
        compiler_params=_cparams(("arbitrary",)),
        name="shortconv",
    )(p_main, p_main, p_main, p_main, p_main, p_main, p_main, conv_w, norm_w)


def _residual_layer_norm(o_ref, x_ref, gate_ref, lng_ref, lnb_ref, alpha, modulated=None):
    rows = o_ref.shape[0]
    for r0 in range(0, rows, LN_ROWS):
        sl = slice(r0, r0 + LN_ROWS)
        r = alpha * x_ref[sl, :] + (1.0 + gate_ref[0]) * o_ref[sl, :]
        y = _layer_norm(r, lng_ref[...], lnb_ref[...])
        o_ref[sl, :] = y
        if modulated is not None:
            h_ref, sc_ref, sh_ref = modulated
            h_ref[sl, :] = (y * (1.0 + sc_ref[0]) + sh_ref[0]).astype(BF16)


def _row_tile_copy(hbm_ref, buf_ref, sem, tm):
    rows = pl.ds(pl.multiple_of(pl.program_id(0) * tm, tm), tm)
    return pltpu.make_async_copy(hbm_ref.at[rows, :], buf_ref, sem)


def _outproj_body(ys_ref, yc_ref, w_ref, x_hbm, gate_ref, sc_ref, sh_ref, lng_ref, lnb_ref, x1_ref, h2_ref,
                  x_buf, sem, *, alpha, tm, tn):
    n = pl.program_id(1)

    @pl.when(n == 0)
    def _():
        _row_tile_copy(x_hbm, x_buf, sem, tm).start()

    k_ssm = ys_ref.shape[1]
    cols = pl.ds(pl.multiple_of(n * tn, tn), tn)
    x1_ref[:, cols] = (jnp.dot(ys_ref[...], w_ref[:k_ssm, :], preferred_element_type=F32)
                       + jnp.dot(yc_ref[...], w_ref[k_ssm:, :], preferred_element_type=F32))

    @pl.when(n == pl.num_programs(1) - 1)
    def _():
        _row_tile_copy(x_hbm, x_buf, sem, tm).wait()
        _residual_layer_norm(x1_ref, x_buf, gate_ref, lng_ref, lnb_ref, alpha,
                             modulated=(h2_ref, sc_ref, sh_ref))


def _outproj(y_ssm, y_sc, w_out, x2, gate, scale2, shift2, ln_g, ln_b, seq, alpha):
    m, d = x2.shape
    tm, tn = 512, 512
    tiles_per_batch = seq // tm
    mod_map = lambda i, n: (i // tiles_per_batch, 0, 0)
    const = lambda i, n: (0, 0)
    row_tile = lambda i, n: (i, 0)
    return pl.pallas_call(
        functools.partial(_outproj_body, alpha=alpha, tm=tm, tn=tn),
        grid=(m // tm, d // tn),
        in_specs=[pl.BlockSpec((tm, D_SSM), row_tile),
                  pl.BlockSpec((tm, D_SC), row_tile),
                  pl.BlockSpec((D_SSM + D_SC, tn), lambda i, n: (0, n)),
                  pl.BlockSpec(memory_space=pl.ANY),
                  pl.BlockSpec((1, 1, d), mod_map),
                  pl.BlockSpec((1, 1, d), mod_map),
                  pl.BlockSpec((1, 1, d), mod_map),
                  pl.BlockSpec((1, d), const),
                  pl.BlockSpec((1, d), const)],
        out_specs=[pl.BlockSpec((tm, d), row_tile),
                   pl.BlockSpec((tm, d), row_tile)],
        out_shape=[jax.ShapeDtypeStruct((m, d), F32),
                   jax.ShapeDtypeStruct((m, d), BF16)],
        scratch_shapes=[pltpu.VMEM((tm, d), F32), pltpu.SemaphoreType.DMA(())],
        compiler_params=_cparams(("arbitrary", "arbitrary")),
        name="outproj",
    )(y_ssm, y_sc, w_out, x2, gate, scale2, shift2, ln_g, ln_b)


def _ffn_body(h2_ref, x1_hbm, gate_ref, wu_ref, wd_ref, lng_ref, lnb_ref, o_ref, x1_buf, sem, *, alpha, tm):
    f = pl.program_id(1)

    @pl.when(f == 0)
    def _():
        _row_tile_copy(x1_hbm, x1_buf, sem, tm).start()
        o_ref[...] = jnp.zeros_like(o_ref)

    u = jnp.maximum(jnp.dot(h2_ref[...], wu_ref[...], preferred_element_type=F32), 0.0)
    u = (u * u).astype(BF16)
    for n0 in range(0, o_ref.shape[1], ACC_COLS):
        cols = slice(n0, n0 + ACC_COLS)
        o_ref[:, cols] += jnp.dot(u, wd_ref[:, cols], preferred_element_type=F32)

    @pl.when(f == pl.num_programs(1) - 1)
    def _():
        _row_tile_copy(x1_hbm, x1_buf, sem, tm).wait()
        _residual_layer_norm(o_ref, x1_buf, gate_ref, lng_ref, lnb_ref, alpha)


def _ffn(h2, x1, gate, w_up, w_down, ln_g, ln_b, seq, alpha):
    m, d = x1.shape
    ff = w_up.shape[1]
    tm, tf = 512, 512
    tiles_per_batch = seq // tm
    mod_map = lambda i, f: (i // tiles_per_batch, 0, 0)
    const = lambda i, f: (0, 0)
    row_tile = lambda i, f: (i, 0)
    return pl.pallas_call(
        functools.partial(_ffn_body, alpha=alpha, tm=tm),
        grid=(m // tm, ff // tf),
        in_specs=[pl.BlockSpec((tm, d), row_tile),
                  pl.BlockSpec(memory_space=pl.ANY),
                  pl.BlockSpec((1, 1, d), mod_map),
                  pl.BlockSpec((d, tf), lambda i, f: (0, f)),
                  pl.BlockSpec((tf, d), lambda i, f: (f, 0)),
                  pl.BlockSpec((1, d), const),
                  pl.BlockSpec((1, d), const)],
        out_specs=pl.BlockSpec((tm, d), row_tile),
        out_shape=jax.ShapeDtypeStruct((m, d), F32),
        scratch_shapes=[pltpu.VMEM((tm, d), F32), pltpu.SemaphoreType.DMA(())],
        compiler_params=_cparams(("arbitrary", "arbitrary")),
        name="ffn",
    )(h2, x1, gate, w_up, w_down, ln_g, ln_b)


def _head_expansion(lane_offset):
    rows = jnp.arange(DT_PAD)[:, None]
    heads = jnp.arange(D_SSM)[None, :] // HEAD_DIM
    e = (rows == heads + lane_offset).astype(BF16)
    return jnp.tile(e, (3, 1))


def _pad_rows(a, rows):
    return jnp.pad(a, ((0, rows - a.shape[0]), (0, 0)))


def kernel(x, c, w_ada, b_ada, w_in, ssm_conv_w, ssm_conv_b, ssm_dt_bias_f, ssm_dt_bias_b, ssm_a_log_f, ssm_a_log_b, ssm_d, ssm_norm_w, sc_conv_w, sc_norm_w, w_out, ln1_g, ln1_b, w_up, w_down, ln2_g, ln2_b):
    batch, seq, d = x.shape
    depth = w_ada.shape[0]
    alpha = float((2 * depth) ** 0.25)
    m = batch * seq
    dt_lo = D_SSM + D_XBC
    dt_hi = dt_lo + 2 * N_HEADS
    e3f = _head_expansion(0)
    e3b = _head_expansion(N_HEADS)
    c_pad = _pad_rows(c, HALO)
    x2 = x.reshape(m, d)

    for l in range(depth):
        mod = _ada(c_pad, w_ada[l], b_ada[l][None, :])[:batch]
        shift1, scale1, gate1, shift2, scale2, gate2 = [
            t[:, None, :] for t in jnp.split(mod, N_MOD, axis=-1)]

        w = w_in[l]
        w_main = jnp.concatenate([w[:, D_SSM:dt_lo], w[:, :D_SSM], w[:, dt_hi:]], axis=1).astype(BF16)
        w_dt = jnp.pad(w[:, dt_lo:dt_hi], ((0, 0), (0, DT_PAD - 2 * N_HEADS))).astype(BF16)
        p_main, dt = _inproj(x2, scale1, shift1, w_main, w_dt, seq)

        zeros_q = jnp.zeros((DT_PAD - 2 * N_HEADS,), F32)
        bias_q = jnp.concatenate([ssm_dt_bias_f[l], ssm_dt_bias_b[l], zeros_q])[None, :]
        alog_q = jnp.concatenate([ssm_a_log_f[l], ssm_a_log_b[l], zeros_q])[None, :]
        act, yoffb, dtq = _ssd_bwd(p_main, dt, _pad_rows(ssm_conv_w[l], HALO), ssm_conv_b[l][None, :],
                                   bias_q, alog_q, e3b, batch, seq)
        dskip_x = jnp.repeat(ssm_d[l], HEAD_DIM)[None, :]
        y_ssm = _ssd_fwd(act, p_main, dtq, yoffb, alog_q, dskip_x, ssm_norm_w[l][None, :], e3f, batch, seq)
        y_sc = _shortconv(p_main, _pad_rows(sc_conv_w[l], HALO), sc_norm_w[l][None, :], seq)

        x1, h2 = _outproj(y_ssm, y_sc, w_out[l].astype(BF16), x2, gate1, scale2, shift2,
                          ln1_g[l][None, :], ln1_b[l][None, :], seq, alpha)
        x2 = _ffn(h2, x1, gate2, w_up[l].astype(BF16), w_down[l].astype(BF16),
                  ln2_g[l][None, :], ln2_b[l][None, :], seq, alpha)
    return x2.reshape(batch, seq, d)
```

```python
import functools

import jax
import jax.numpy as jnp
from jax import lax
from jax.experimental import pallas as pl
from jax.experimental.pallas import tpu as pltpu

F32 = jnp.float32
BF16 = jnp.bfloat16
HIGHEST = lax.Precision.HIGHEST

D_MODEL = 4096
D_SSM = 2048
D_SC = 2048
HEAD_DIM = 64
N_HEADS = 32
N_GROUPS = 8
HEADS_PER_GROUP = 4
D_STATE = 128
CHUNK = 128
SSM_CONV = 5
SC_CONV = 3
SC_GROUPS = 16
D_XBC = D_SSM + 2 * N_GROUPS * D_STATE
D_FF = 4 * D_MODEL
N_MOD = 6
LN_EPS = 1e-5
RMS_EPS = 1e-5
GROUP_W = HEADS_PER_GROUP * HEAD_DIM
D_MAIN = D_XBC + D_SSM + 3 * D_SC
DT_PAD = 128
HALO = 8
HALO_BF16 = 16
LANES = 128
LN_ROWS = 64
ACC_COLS = 512

COL_Z = 0
COL_H = 1
COL_B = 2
COL_C = 3

VMEM_LIMIT = 60 * 1024 * 1024


def _cparams(sem):
    return pltpu.CompilerParams(dimension_semantics=sem, vmem_limit_bytes=VMEM_LIMIT)


def _sigmoid(v):
    return 1.0 / (1.0 + jnp.exp(-v))


def _silu(v):
    return v * _sigmoid(v)


def _softplus(v):
    return jnp.maximum(v, 0.0) + jnp.log1p(jnp.exp(-jnp.abs(v)))


def _layer_norm(r, g, b):
    mu = jnp.mean(r, axis=-1, keepdims=True)
    d = r - mu
    var = jnp.mean(d * d, axis=-1, keepdims=True)
    return d * lax.rsqrt(var + LN_EPS) * g + b


def _ada_body(c_ref, w_ref, b_ref, o_ref):
    o_ref[...] = jnp.dot(_silu(c_ref[...]), w_ref[...], preferred_element_type=F32,
                         precision=HIGHEST) + b_ref[...]


def _ada(c_pad, w_ada, b_ada):
    rows, d = c_pad.shape
    n = w_ada.shape[1]
    tn = 512
    return pl.pallas_call(
        _ada_body,
        grid=(n // tn,),
        in_specs=[pl.BlockSpec((rows, d), lambda j: (0, 0)),
                  pl.BlockSpec((d, tn), lambda j: (0, j)),
                  pl.BlockSpec((1, tn), lambda j: (0, j))],
        out_specs=pl.BlockSpec((rows, tn), lambda j: (0, j)),
        out_shape=jax.ShapeDtypeStruct((rows, n), F32),
        compiler_params=_cparams(("arbitrary",)),
        name="ada",
    )(c_pad, w_ada, b_ada)


def _regroup_body(w_ref, main_ref, dt_ref):
    dt_lo = D_SSM + D_XBC
    dt_hi = dt_lo + 2 * N_HEADS
    main_ref[:, :D_XBC] = w_ref[:, D_SSM:dt_lo].astype(BF16)
    main_ref[:, D_XBC:D_XBC + D_SSM] = w_ref[:, :D_SSM].astype(BF16)
    main_ref[:, D_XBC + D_SSM:] = w_ref[:, dt_hi:].astype(BF16)
    dt_ref[...] = jnp.zeros_like(dt_ref)
    dt_ref[:, :2 * N_HEADS] = w_ref[:, dt_lo:dt_hi].astype(BF16)


def _regroup_w_in(w):
    d, n = w.shape
    rows = 256
    return pl.pallas_call(
        _regroup_body,
        grid=(d // rows,),
        in_specs=[pl.BlockSpec((rows, n), lambda i: (i, 0))],
        out_specs=[pl.BlockSpec((rows, D_MAIN), lambda i: (i, 0)),
                   pl.BlockSpec((rows, DT_PAD), lambda i: (i, 0))],
        out_shape=[jax.ShapeDtypeStruct((d, D_MAIN), BF16),
                   jax.ShapeDtypeStruct((d, DT_PAD), BF16)],
        compiler_params=_cparams(("arbitrary",)),
        name="regroup_w_in",
    )(w)


def _inproj_body(x_ref, sc_ref, sh_ref, w_ref, wdt_ref, xbc_ref, rest_ref, dt_ref, h_scr, *, xbc_tiles):
    j = pl.program_id(1)

    @pl.when(j == 0)
    def _():
        h = (x_ref[...] * (1.0 + sc_ref[0]) + sh_ref[0]).astype(BF16)
        h_scr[...] = h
        dt_ref[...] = jnp.dot(h, wdt_ref[...], preferred_element_type=F32)

    @pl.when(j < xbc_tiles)
    def _():
        xbc_ref[...] = jnp.dot(h_scr[...], w_ref[...], preferred_element_type=F32)

    @pl.when(j >= xbc_tiles)
    def _():
        rest_ref[...] = jnp.dot(h_scr[...], w_ref[...], preferred_element_type=F32).astype(BF16)


def _inproj(x2, scale, shift, w_main, w_dt, seq):
    m, d = x2.shape
    n = w_main.shape[1]
    tm, tn = 1024, 512
    tiles_per_batch = seq // tm
    xbc_tiles = D_XBC // tn
    return pl.pallas_call(
        functools.partial(_inproj_body, xbc_tiles=xbc_tiles),
        grid=(m // tm, n // tn),
        in_specs=[pl.BlockSpec((tm, d), lambda i, j: (i, 0)),
                  pl.BlockSpec((1, 1, d), lambda i, j: (i // tiles_per_batch, 0, 0)),
                  pl.BlockSpec((1, 1, d), lambda i, j: (i // tiles_per_batch, 0, 0)),
                  pl.BlockSpec((d, tn), lambda i, j: (0, j)),
                  pl.BlockSpec((d, DT_PAD), lambda i, j: (0, 0))],
        out_specs=[pl.BlockSpec((tm, tn), lambda i, j: (i, jnp.minimum(j, xbc_tiles - 1))),
                   pl.BlockSpec((tm, tn), lambda i, j: (i, jnp.maximum(j - xbc_tiles, 0))),
                   pl.BlockSpec((tm, DT_PAD), lambda i, j: (i, 0))],
        out_shape=[jax.ShapeDtypeStruct((m, D_XBC), F32),
                   jax.ShapeDtypeStruct((m, n - D_XBC), BF16),
                   jax.ShapeDtypeStruct((m, DT_PAD), F32)],
        scratch_shapes=[pltpu.VMEM((tm, d), BF16)],
        compiler_params=_cparams(("arbitrary", "arbitrary")),
        name="inproj",
    )(x2, scale, shift, w_main, w_dt)


def _expand_heads(q, e3_ref):
    hi = q.astype(BF16)
    r1 = q - hi.astype(F32)
    mid = r1.astype(BF16)
    lo = (r1 - mid.astype(F32)).astype(BF16)
    pieces = jnp.concatenate([hi, mid, lo], axis=1)
    return jnp.dot(pieces, e3_ref[...], preferred_element_type=F32)


def _chunk_decay_terms(dtq, alog_ref):
    row = lax.broadcasted_iota(jnp.int32, (CHUNK, CHUNK), 0)
    col = lax.broadcasted_iota(jnp.int32, (CHUNK, CHUNK), 1)
    tril = col <= row
    triu = col >= row
    da = dtq * (-jnp.exp(alog_ref[...]))
    pre = jnp.dot(tril.astype(F32), da, preferred_element_type=F32, precision=HIGHEST)
    suf = jnp.dot(triu.astype(F32), da, preferred_element_type=F32, precision=HIGHEST)
    acum = jnp.where(col < N_HEADS, pre, suf)
    return acum, tril, triu


def _ssd_bwd_body(cur_ref, prev_ref, next_ref, dt_ref, cw_ref, cb_ref, bias_ref, alog_ref, e3_ref,
                  act_ref, yoff_ref, dtq_ref, st_ref, *, n_chunks):
    s = pl.program_id(1)
    c = n_chunks - 1 - s

    @pl.when(s == 0)
    def _():
        st_ref[...] = jnp.zeros_like(st_ref)

    lane_tile = 512
    rows = CHUNK + 2 * HALO
    has_prev = c > 0
    has_next = c < n_chunks - 1
    for t in range(D_XBC // lane_tile):
        sl = slice(t * lane_tile, (t + 1) * lane_tile)
        stack = jnp.concatenate([jnp.where(has_prev, prev_ref[:, sl], 0.0), cur_ref[:, sl],
                                 jnp.where(has_next, next_ref[:, sl], 0.0)], axis=0)
        w = cw_ref[:, sl]
        conv = cb_ref[:, sl] + w[SSM_CONV // 2:SSM_CONV // 2 + 1, :] * stack[HALO:HALO + CHUNK, :]
        for k in range(SSM_CONV):
            shift = SSM_CONV // 2 - k
            if shift != 0:
                tap = pltpu.roll(stack, shift % rows, 0)[HALO:HALO + CHUNK, :]
                conv = conv + w[k:k + 1, :] * tap
        act_ref[:, sl] = _silu(conv)

    dtq = _softplus(dt_ref[...] + bias_ref[...])
    dtq_ref[...] = dtq
    acum, _, _ = _chunk_decay_terms(dtq, alog_ref)
    lane = lax.broadcasted_iota(jnp.int32, (CHUNK, DT_PAD), 1)
    acum = jnp.where((lane >= N_HEADS) & (lane < 2 * N_HEADS), acum, 0.0)
    total = acum[0:1, :]
    eb_x = _expand_heads(jnp.exp(acum), e3_ref)
    wb_x = _expand_heads(dtq * jnp.exp(total - acum), e3_ref)
    dec_x = eb_x[0:1, :]

    for g in range(N_GROUPS):
        ch = slice(g * GROUP_W, (g + 1) * GROUP_W)
        b_g = act_ref[:, D_SSM + g * D_STATE:D_SSM + (g + 1) * D_STATE].astype(BF16)
        c_g = act_ref[:, D_SSM + N_GROUPS * D_STATE + g * D_STATE:
                      D_SSM + N_GROUPS * D_STATE + (g + 1) * D_STATE].astype(BF16)
        st = st_ref[g]
        yoff_ref[:, ch] = jnp.dot(c_g, st.astype(BF16), preferred_element_type=F32) * eb_x[:, ch]
        xw = (act_ref[:, ch] * wb_x[:, ch]).astype(BF16)
        upd = lax.dot_general(b_g, xw, (((0,), (0,)), ((), ())), preferred_element_type=F32)
        st_ref[g] = st * dec_x[:, ch] + upd


def _ssd_bwd(p_main, dt, conv_w, conv_b, bias_q, alog_q, e3b, batch, seq):
    m = p_main.shape[0]
    nc = seq // CHUNK
    rb = CHUNK // HALO
    last_halo = m // HALO - 1

    def cur_map(b, s):
        return (b * nc + (nc - 1 - s), 0)

    def prev_map(b, s):
        return (jnp.maximum((b * nc + (nc - 1 - s)) * rb - 1, 0), 0)

    def next_map(b, s):
        return (jnp.minimum((b * nc + (nc - 1 - s) + 1) * rb, last_halo), 0)

    const = lambda b, s: (0, 0)
    return pl.pallas_call(
        functools.partial(_ssd_bwd_body, n_chunks=nc),
        grid=(batch, nc),
        in_specs=[pl.BlockSpec((CHUNK, D_XBC), cur_map),
                  pl.BlockSpec((HALO, D_XBC), prev_map),
                  pl.BlockSpec((HALO, D_XBC), next_map),
                  pl.BlockSpec((CHUNK, DT_PAD), cur_map),
                  pl.BlockSpec((HALO, D_XBC), const),
                  pl.BlockSpec((1, D_XBC), const),
                  pl.BlockSpec((1, DT_PAD), const),
                  pl.BlockSpec((1, DT_PAD), const),
                  pl.BlockSpec((3 * DT_PAD, D_SSM), const)],
        out_specs=[pl.BlockSpec((CHUNK, D_XBC), cur_map),
                   pl.BlockSpec((CHUNK, D_SSM), cur_map),
                   pl.BlockSpec((CHUNK, DT_PAD), cur_map)],
        out_shape=[jax.ShapeDtypeStruct((m, D_XBC), F32),
                   jax.ShapeDtypeStruct((m, D_SSM), F32),
                   jax.ShapeDtypeStruct((m, DT_PAD), F32)],
        scratch_shapes=[pltpu.VMEM((N_GROUPS, D_STATE, GROUP_W), F32)],
        compiler_params=_cparams(("arbitrary", "arbitrary")),
        name="ssd_bwd",
    )(p_main, p_main, p_main, dt, conv_w, conv_b, bias_q, alog_q, e3b)


def _ssd_fwd_body(act_ref, z_ref, dtq_ref, yoffb_ref, alog_ref, dskip_ref, normw_ref, e3_ref,
                  y_ref, st_ref):
    @pl.when(pl.program_id(1) == 0)
    def _():
        st_ref[...] = jnp.zeros_like(st_ref)

    dtq = dtq_ref[...]
    acum, tril, triu = _chunk_decay_terms(dtq, alog_ref)
    acum_t = acum.T
    dtq_t = dtq.T
    lane = lax.broadcasted_iota(jnp.int32, (CHUNK, DT_PAD), 1)
    acum_f = jnp.where(lane < N_HEADS, acum, 0.0)
    total = acum_f[CHUNK - 1:CHUNK, :]
    ef_x = _expand_heads(jnp.exp(acum_f), e3_ref)
    wf_x = _expand_heads(dtq * jnp.exp(total - acum_f), e3_ref)
    dec_x = ef_x[CHUNK - 1:CHUNK, :]
    head_of_lane = lax.broadcasted_iota(jnp.int32, (CHUNK, GROUP_W), 1) // HEAD_DIM
    neg_inf = jnp.float32(-jnp.inf)

    for g in range(N_GROUPS):
        ch = slice(g * GROUP_W, (g + 1) * GROUP_W)
        b_g = act_ref[:, D_SSM + g * D_STATE:D_SSM + (g + 1) * D_STATE].astype(BF16)
        c_g = act_ref[:, D_SSM + N_GROUPS * D_STATE + g * D_STATE:
                      D_SSM + N_GROUPS * D_STATE + (g + 1) * D_STATE].astype(BF16)
        x_g = act_ref[:, ch]
        scores = lax.dot_general(c_g, b_g, (((1,), (1,)), ((), ())), preferred_element_type=F32)

        mats = []
        blocks = []
        for r in range(HEADS_PER_GROUP):
            hf = g * HEADS_PER_GROUP + r
            hb = N_HEADS + hf
            seg_f = acum[:, hf:hf + 1] - acum_t[hf:hf + 1, :]
            seg_b = acum[:, hb:hb + 1] - acum_t[hb:hb + 1, :]
            lf = jnp.exp(jnp.where(tril, seg_f, neg_inf)) * dtq_t[hf:hf + 1, :]
            ub = jnp.exp(jnp.where(triu, seg_b, neg_inf)) * dtq_t[hb:hb + 1, :]
            mats.append((scores * (lf + ub)).astype(BF16))
            blocks.append(jnp.where(head_of_lane == r, x_g, 0.0).astype(BF16))
        y_diag = jnp.dot(jnp.concatenate(mats, axis=1), jnp.concatenate(blocks, axis=0),
                         preferred_element_type=F32)

        st = st_ref[g]
        y_off = jnp.dot(c_g, st.astype(BF16), preferred_element_type=F32) * ef_x[:, ch]
        y = y_diag + y_off + yoffb_ref[:, ch] + dskip_ref[:, ch] * x_g

        xw = (x_g * wf_x[:, ch]).astype(BF16)
        upd = lax.dot_general(b_g, xw, (((0,), (0,)), ((), ())), preferred_element_type=F32)
        st_ref[g] = st * dec_x[:, ch] + upd

        y = y * _silu(z_ref[:, ch].astype(F32))
        ms = jnp.mean(y * y, axis=-1, keepdims=True)
        y_ref[:, ch] = (y * lax.rsqrt(ms + RMS_EPS) * normw_ref[:, ch]).astype(BF16)


def _ssd_fwd(act, p_rest, dtq, yoffb, alog_q, dskip_x, norm_w, e3f, batch, seq):
    m = act.shape[0]
    nc = seq // CHUNK
    cur = lambda b, s: (b * nc + s, 0)
    const = lambda b, s: (0, 0)
    return pl.pallas_call(
        _ssd_fwd_body,
        grid=(batch, nc),
        in_specs=[pl.BlockSpec((CHUNK, D_XBC), cur),
                  pl.BlockSpec((CHUNK, D_SSM), lambda b, s: (b * nc + s, COL_Z)),
                  pl.BlockSpec((CHUNK, DT_PAD), cur),
                  pl.BlockSpec((CHUNK, D_SSM), cur),
                  pl.BlockSpec((1, DT_PAD), const),
                  pl.BlockSpec((1, D_SSM), const),
                  pl.BlockSpec((1, D_SSM), const),
                  pl.BlockSpec((3 * DT_PAD, D_SSM), const)],
        out_specs=pl.BlockSpec((CHUNK, D_SSM), cur),
        out_shape=jax.ShapeDtypeStruct((m, D_SSM), BF16),
        scratch_shapes=[pltpu.VMEM((N_GROUPS, D_STATE, GROUP_W), F32)],
        compiler_params=_cparams(("arbitrary", "arbitrary")),
        name="ssd_fwd",
    )(act, p_rest, dtq, yoffb, alog_q, dskip_x, norm_w, e3f)


def _shortconv_body(h_ref, b_ref, c_ref, hp_ref, cp_ref, hn_ref, cn_ref, cw_ref, nw_ref, y_ref,
                    *, tiles_per_seq, rows):
    i = pl.program_id(0)
    t = i % tiles_per_seq
    has_prev = t > 0
    has_next = t < tiles_per_seq - 1
    lane_tile = 512
    row = lax.broadcasted_iota(jnp.int32, (rows, lane_tile), 0)
    group = D_SC // SC_GROUPS
    for q in range(D_SC // lane_tile):
        sl = slice(q * lane_tile, (q + 1) * lane_tile)
        v = c_ref[:, sl].astype(F32) * h_ref[:, sl].astype(F32)
        last = HALO_BF16 - 1
        vp = jnp.where(has_prev, cp_ref[:, sl].astype(F32)[last:, :] * hp_ref[:, sl].astype(F32)[last:, :], 0.0)
        vn = jnp.where(has_next, cn_ref[:, sl].astype(F32)[0:1, :] * hn_ref[:, sl].astype(F32)[0:1, :], 0.0)
        down = jnp.where(row == 0, vp, pltpu.roll(v, 1, 0))
        up = jnp.where(row == rows - 1, vn, pltpu.roll(v, rows - 1, 0))
        w = cw_ref[:, sl]
        y = b_ref[:, sl].astype(F32) * (w[0:1, :] * down + w[1:2, :] * v + w[2:3, :] * up)
        for k in range(lane_tile // group):
            gs = slice(k * group, (k + 1) * group)
            yg = y[:, gs]
            ms = jnp.mean(yg * yg, axis=-1, keepdims=True)
            lo = q * lane_tile + k * group
            y_ref[:, lo:lo + group] = (yg * lax.rsqrt(ms + RMS_EPS) * nw_ref[:, lo:lo + group]).astype(BF16)


def _shortconv(p_rest, conv_w, norm_w, seq):
    m = p_rest.shape[0]
    rows = 256
    tiles_per_seq = seq // rows
    rb = rows // HALO_BF16
    last_halo = m // HALO_BF16 - 1

    def cur(col):
        return lambda i: (i, col)

    def prev(col):
        return lambda i: (jnp.maximum(i * rb - 1, 0), col)

    def nxt(col):
        return lambda i: (jnp.minimum((i + 1) * rb, last_halo), col)

    const = lambda i: (0, 0)
    return pl.pallas_call(
        functools.partial(_shortconv_body, tiles_per_seq=tiles_per_seq, rows=rows),
        grid=(m // rows,),
        in_specs=[pl.BlockSpec((rows, D_SC), cur(COL_H)),
                  pl.BlockSpec((rows, D_SC), cur(COL_B)),
                  pl.BlockSpec((rows, D_SC), cur(COL_C)),
                  pl.BlockSpec((HALO_BF16, D_SC), prev(COL_H)),
                  pl.BlockSpec((HALO_BF16, D_SC), prev(COL_C)),
                  pl.BlockSpec((HALO_BF16, D_SC), nxt(COL_H)),
                  pl.BlockSpec((HALO_BF16, D_SC), nxt(COL_C)),
                  pl.BlockSpec((HALO, D_SC), const),
                  pl.BlockSpec((1, D_SC), const)],
        out_specs=pl.BlockSpec((rows, D_SC), lambda i: (i, 0)),
        out_shape=jax.ShapeDtypeStruct((m, D_SC), BF16),
        compiler_params=_cparams(("arbitrary",)),
        name="shortconv",
    )(p_rest, p_rest, p_rest, p_rest, p_rest, p_rest, p_rest, conv_w, norm_w)


def _residual_layer_norm(o_ref, x_ref, gate_ref, lng_ref, lnb_ref, alpha, modulated=None):
    width = o_ref.shape[1]
    lane_tiles = [slice(t, t + LANES) for t in range(0, width, LANES)]

    def slab(i, carry):
        sl = pl.ds(pl.multiple_of(i * LN_ROWS, LN_ROWS), LN_ROWS)
        acc = jnp.zeros((LN_ROWS, LANES), F32)
        for lt in lane_tiles:
            r = alpha * x_ref[sl, lt] + (1.0 + gate_ref[0, :, lt]) * o_ref[sl, lt]
            o_ref[sl, lt] = r
            acc = acc + r
        mean = jnp.broadcast_to(jnp.sum(acc, axis=-1, keepdims=True) * (1.0 / width), (LN_ROWS, LANES))
        acc = jnp.zeros((LN_ROWS, LANES), F32)
        for lt in lane_tiles:
            d = o_ref[sl, lt] - mean
            acc = acc + d * d
        var = jnp.sum(acc, axis=-1, keepdims=True) * (1.0 / width)
        rstd = jnp.broadcast_to(lax.rsqrt(var + LN_EPS), (LN_ROWS, LANES))
        for lt in lane_tiles:
            y = (o_ref[sl, lt] - mean) * rstd * lng_ref[:, lt] + lnb_ref[:, lt]
            o_ref[sl, lt] = y
            if modulated is not None:
                h_ref, sc_ref, sh_ref = modulated
                h_ref[sl, lt] = (y * (1.0 + sc_ref[0, :, lt]) + sh_ref[0, :, lt]).astype(BF16)
        return carry

    lax.fori_loop(0, o_ref.shape[0] // LN_ROWS, slab, 0)


def _row_tile_copy(hbm_ref, buf_ref, sem, tm):
    rows = pl.ds(pl.multiple_of(pl.program_id(0) * tm, tm), tm)
    return pltpu.make_async_copy(hbm_ref.at[rows, :], buf_ref, sem)


def _outproj_body(ys_ref, yc_ref, w_ref, x_hbm, gate_ref, sc_ref, sh_ref, lng_ref, lnb_ref, x1_ref, h2_ref,
                  x_buf, sem, *, alpha, tm, tn):
    n = pl.program_id(1)

    @pl.when(n == 0)
    def _():
        _row_tile_copy(x_hbm, x_buf, sem, tm).start()

    k_ssm = ys_ref.shape[1]
    cols = pl.ds(pl.multiple_of(n * tn, tn), tn)
    x1_ref[:, cols] = (jnp.dot(ys_ref[...], w_ref[:k_ssm, :], preferred_element_type=F32)
                       + jnp.dot(yc_ref[...], w_ref[k_ssm:, :], preferred_element_type=F32))

    @pl.when(n == pl.num_programs(1) - 1)
    def _():
        _row_tile_copy(x_hbm, x_buf, sem, tm).wait()
        _residual_layer_norm(x1_ref, x_buf, gate_ref, lng_ref, lnb_ref, alpha,
                             modulated=(h2_ref, sc_ref, sh_ref))


def _outproj(y_ssm, y_sc, w_out, x2, gate, scale2, shift2, ln_g, ln_b, seq, alpha):
    m, d = x2.shape
    tm, tn = 512, 512
    tiles_per_batch = seq // tm
    mod_map = lambda i, n: (i // tiles_per_batch, 0, 0)
    const = lambda i, n: (0, 0)
    row_tile = lambda i, n: (i, 0)
    return pl.pallas_call(
        functools.partial(_outproj_body, alpha=alpha, tm=tm, tn=tn),
        grid=(m // tm, d // tn),
        in_specs=[pl.BlockSpec((tm, D_SSM), row_tile),
                  pl.BlockSpec((tm, D_SC), row_tile),
                  pl.BlockSpec((D_SSM + D_SC, tn), lambda i, n: (0, n)),
                  pl.BlockSpec(memory_space=pl.ANY),
                  pl.BlockSpec((1, 1, d), mod_map),
                  pl.BlockSpec((1, 1, d), mod_map),
                  pl.BlockSpec((1, 1, d), mod_map),
                  pl.BlockSpec((1, d), const),
                  pl.BlockSpec((1, d), const)],
        out_specs=[pl.BlockSpec((tm, d), row_tile),
                   pl.BlockSpec((tm, d), row_tile)],
        out_shape=[jax.ShapeDtypeStruct((m, d), F32),
                   jax.ShapeDtypeStruct((m, d), BF16)],
        scratch_shapes=[pltpu.VMEM((tm, d), F32), pltpu.SemaphoreType.DMA(())],
        compiler_params=_cparams(("arbitrary", "arbitrary")),
        name="outproj",
    )(y_ssm, y_sc, w_out, x2, gate, scale2, shift2, ln_g, ln_b)


def _ffn_body(h2_ref, x1_hbm, gate_ref, wu_ref, wd_ref, lng_ref, lnb_ref, o_ref, x1_buf, sem, *, alpha, tm):
    f = pl.program_id(1)

    @pl.when(f == 0)
    def _():
        _row_tile_copy(x1_hbm, x1_buf, sem, tm).start()
        o_ref[...] = jnp.zeros_like(o_ref)

    u = jnp.maximum(jnp.dot(h2_ref[...], wu_ref[...], preferred_element_type=F32), 0.0)
    u = (u * u).astype(BF16)
    for n0 in range(0, o_ref.shape[1], ACC_COLS):
        cols = slice(n0, n0 + ACC_COLS)
        o_ref[:, cols] += jnp.dot(u, wd_ref[:, cols], preferred_element_type=F32)

    @pl.when(f == pl.num_programs(1) - 1)
    def _():
        _row_tile_copy(x1_hbm, x1_buf, sem, tm).wait()
        _residual_layer_norm(o_ref, x1_buf, gate_ref, lng_ref, lnb_ref, alpha)


def _ffn(h2, x1, gate, w_up, w_down, ln_g, ln_b, seq, alpha):
    m, d = x1.shape
    ff = w_up.shape[1]
    tm, tf = 512, 512
    tiles_per_batch = seq // tm
    mod_map = lambda i, f: (i // tiles_per_batch, 0, 0)
    const = lambda i, f: (0, 0)
    row_tile = lambda i, f: (i, 0)
    return pl.pallas_call(
        functools.partial(_ffn_body, alpha=alpha, tm=tm),
        grid=(m // tm, ff // tf),
        in_specs=[pl.BlockSpec((tm, d), row_tile),
                  pl.BlockSpec(memory_space=pl.ANY),
                  pl.BlockSpec((1, 1, d), mod_map),
                  pl.BlockSpec((d, tf), lambda i, f: (0, f)),
                  pl.BlockSpec((tf, d), lambda i, f: (f, 0)),
                  pl.BlockSpec((1, d), const),
                  pl.BlockSpec((1, d), const)],
        out_specs=pl.BlockSpec((tm, d), row_tile),
        out_shape=jax.ShapeDtypeStruct((m, d), F32),
        scratch_shapes=[pltpu.VMEM((tm, d), F32), pltpu.SemaphoreType.DMA(())],
        compiler_params=_cparams(("arbitrary", "arbitrary")),
        name="ffn",
    )(h2, x1, gate, w_up, w_down, ln_g, ln_b)


def _head_expansion(lane_offset):
    rows = jnp.arange(DT_PAD)[:, None]
    heads = jnp.arange(D_SSM)[None, :] // HEAD_DIM
    e = (rows == heads + lane_offset).astype(BF16)
    return jnp.tile(e, (3, 1))


def _pad_rows(a, rows):
    return jnp.pad(a, ((0, rows - a.shape[0]), (0, 0)))


def kernel(x, c, w_ada, b_ada, w_in, ssm_conv_w, ssm_conv_b, ssm_dt_bias_f, ssm_dt_bias_b, ssm_a_log_f, ssm_a_log_b, ssm_d, ssm_norm_w, sc_conv_w, sc_norm_w, w_out, ln1_g, ln1_b, w_up, w_down, ln2_g, ln2_b):
    batch, seq, d = x.shape
    depth = w_ada.shape[0]
    alpha = float((2 * depth) ** 0.25)
    m = batch * seq
    dt_lo = D_SSM + D_XBC
    dt_hi = dt_lo + 2 * N_HEADS
    e3f = _head_expansion(0)
    e3b = _head_expansion(N_HEADS)
    c_pad = _pad_rows(c, HALO)
    x2 = x.reshape(m, d)

    for l in range(depth):
        mod = _ada(c_pad, w_ada[l], b_ada[l][None, :])[:batch]
        shift1, scale1, gate1, shift2, scale2, gate2 = [
            t[:, None, :] for t in jnp.split(mod, N_MOD, axis=-1)]

        w_main, w_dt = _regroup_w_in(w_in[l])
        p_xbc, p_rest, dt = _inproj(x2, scale1, shift1, w_main, w_dt, seq)

        zeros_q = jnp.zeros((DT_PAD - 2 * N_HEADS,), F32)
        bias_q = jnp.concatenate([ssm_dt_bias_f[l], ssm_dt_bias_b[l], zeros_q])[None, :]
        alog_q = jnp.concatenate([ssm_a_log_f[l], ssm_a_log_b[l], zeros_q])[None, :]
        act, yoffb, dtq = _ssd_bwd(p_xbc, dt, _pad_rows(ssm_conv_w[l], HALO), ssm_conv_b[l][None, :],
                                   bias_q, alog_q, e3b, batch, seq)
        dskip_x = jnp.repeat(ssm_d[l], HEAD_DIM)[None, :]
        y_ssm = _ssd_fwd(act, p_rest, dtq, yoffb, alog_q, dskip_x, ssm_norm_w[l][None, :], e3f, batch, seq)
        y_sc = _shortconv(p_rest, _pad_rows(sc_conv_w[l], HALO), sc_norm_w[l][None, :], seq)

        x1, h2 = _outproj(y_ssm, y_sc, w_out[l].astype(BF16), x2, gate1, scale2, shift2,
                          ln1_g[l][None, :], ln1_b[l][None, :], seq, alpha)
        x2 = _ffn(h2, x1, gate2, w_up[l].astype(BF16), w_down[l].astype(BF16),
                  ln2_g[l][None, :], ln2_b[l][None, :], seq, alpha)
    return x2.reshape(batch, seq, d)
```

```python
import functools

import jax
import jax.numpy as jnp
from jax import lax
from jax.experimental import pallas as pl
from jax.experimental.pallas import tpu as pltpu

F32 = jnp.float32
BF16 = jnp.bfloat16
HIGHEST = lax.Precision.HIGHEST

D_MODEL = 4096
D_SSM = 2048
D_SC = 2048
HEAD_DIM = 64
N_HEADS = 32
N_GROUPS = 8
HEADS_PER_GROUP = 4
D_STATE = 128
CHUNK = 128
SSM_CONV = 5
SC_CONV = 3
SC_GROUPS = 16
D_XBC = D_SSM + 2 * N_GROUPS * D_STATE
D_FF = 4 * D_MODEL
N_MOD = 6
LN_EPS = 1e-5
RMS_EPS = 1e-5
GROUP_W = HEADS_PER_GROUP * HEAD_DIM
D_MAIN = D_XBC + D_SSM + 3 * D_SC
DT_PAD = 128
HALO = 8
HALO_BF16 = 16
LANES = 128
LN_ROWS = 64
ACC_COLS = 512

COL_Z = 0
COL_H = 1
COL_B = 2
COL_C = 3

VMEM_LIMIT = 62 * 1024 * 1024


def _cparams(sem):
    return pltpu.CompilerParams(dimension_semantics=sem, vmem_limit_bytes=VMEM_LIMIT)


def _sigmoid(v):
    return 1.0 / (1.0 + jnp.exp(-v))


def _silu(v):
    return v * _sigmoid(v)


def _softplus(v):
    return jnp.maximum(v, 0.0) + jnp.log1p(jnp.exp(-jnp.abs(v)))


def _layer_norm(r, g, b):
    mu = jnp.mean(r, axis=-1, keepdims=True)
    d = r - mu
    var = jnp.mean(d * d, axis=-1, keepdims=True)
    return d * lax.rsqrt(var + LN_EPS) * g + b


def _ada_body(c_ref, w_ref, b_ref, o_ref):
    o_ref[...] = jnp.dot(_silu(c_ref[...]), w_ref[...], preferred_element_type=F32,
                         precision=HIGHEST) + b_ref[...]


def _ada(c_pad, w_ada, b_ada):
    rows, d = c_pad.shape
    n = w_ada.shape[1]
    tn = 512
    return pl.pallas_call(
        _ada_body,
        grid=(n // tn,),
        in_specs=[pl.BlockSpec((rows, d), lambda j: (0, 0)),
                  pl.BlockSpec((d, tn), lambda j: (0, j)),
                  pl.BlockSpec((1, tn), lambda j: (0, j))],
        out_specs=pl.BlockSpec((rows, tn), lambda j: (0, j)),
        out_shape=jax.ShapeDtypeStruct((rows, n), F32),
        compiler_params=_cparams(("arbitrary",)),
        name="ada",
    )(c_pad, w_ada, b_ada)


def _dot_nt(a, b_t):
    return lax.dot_general(a, b_t, (((1,), (1,)), ((), ())), preferred_element_type=F32)


def _inproj_body(x_ref, sc_ref, sh_ref, w_ref, wdt_ref, xbc_ref, rest_ref, dt_ref, h_scr, *, xbc_tiles):
    j = pl.program_id(1)

    @pl.when(j == 0)
    def _():
        h = (x_ref[...] * (1.0 + sc_ref[0]) + sh_ref[0]).astype(BF16)
        h_scr[...] = h
        dt_ref[...] = _dot_nt(h, wdt_ref[...])

    @pl.when(j < xbc_tiles)
    def _():
        xbc_ref[...] = _dot_nt(h_scr[...], w_ref[...])

    @pl.when(j >= xbc_tiles)
    def _():
        rest_ref[...] = _dot_nt(h_scr[...], w_ref[...]).astype(BF16)


def _inproj(x2, scale, shift, w_t, w_dt_t, seq):
    m, d = x2.shape
    n = D_MAIN
    tm, tn = 1024, 512
    tiles_per_batch = seq // tm
    xbc_tiles = D_XBC // tn
    z_tiles = D_SSM // tn
    hbc_start = D_SSM + D_XBC + 2 * N_HEADS

    def w_row_start(j):
        start = jnp.where(j < xbc_tiles, D_SSM + j * tn,
                          jnp.where(j < xbc_tiles + z_tiles, (j - xbc_tiles) * tn,
                                    hbc_start + (j - xbc_tiles - z_tiles) * tn))
        return pl.multiple_of(start, 2 * N_HEADS)
    return pl.pallas_call(
        functools.partial(_inproj_body, xbc_tiles=xbc_tiles),
        grid=(m // tm, n // tn),
        in_specs=[pl.BlockSpec((tm, d), lambda i, j: (i, 0)),
                  pl.BlockSpec((1, 1, d), lambda i, j: (i // tiles_per_batch, 0, 0)),
                  pl.BlockSpec((1, 1, d), lambda i, j: (i // tiles_per_batch, 0, 0)),
                  pl.BlockSpec((pl.Element(tn), pl.Element(d)), lambda i, j: (w_row_start(j), 0)),
                  pl.BlockSpec((DT_PAD, d), lambda i, j: (0, 0))],
        out_specs=[pl.BlockSpec((tm, tn), lambda i, j: (i, jnp.minimum(j, xbc_tiles - 1))),
                   pl.BlockSpec((tm, tn), lambda i, j: (i, jnp.maximum(j - xbc_tiles, 0))),
                   pl.BlockSpec((tm, DT_PAD), lambda i, j: (i, 0))],
        out_shape=[jax.ShapeDtypeStruct((m, D_XBC), F32),
                   jax.ShapeDtypeStruct((m, n - D_XBC), BF16),
                   jax.ShapeDtypeStruct((m, DT_PAD), F32)],
        scratch_shapes=[pltpu.VMEM((tm, d), BF16)],
        compiler_params=_cparams(("arbitrary", "arbitrary")),
        name="inproj",
    )(x2, scale, shift, w_t, w_dt_t)


def _expand_heads(q, e3_ref):
    hi = q.astype(BF16)
    r1 = q - hi.astype(F32)
    mid = r1.astype(BF16)
    lo = (r1 - mid.astype(F32)).astype(BF16)
    pieces = jnp.concatenate([hi, mid, lo], axis=1)
    return jnp.dot(pieces, e3_ref[...], preferred_element_type=F32)


def _chunk_decay_terms(dtq, alog_ref):
    row = lax.broadcasted_iota(jnp.int32, (CHUNK, CHUNK), 0)
    col = lax.broadcasted_iota(jnp.int32, (CHUNK, CHUNK), 1)
    tril = col <= row
    triu = col >= row
    da = dtq * (-jnp.exp(alog_ref[...]))
    pre = jnp.dot(tril.astype(F32), da, preferred_element_type=F32, precision=HIGHEST)
    suf = jnp.dot(triu.astype(F32), da, preferred_element_type=F32, precision=HIGHEST)
    acum = jnp.where(col < N_HEADS, pre, suf)
    return acum, tril, triu


def _ssd_bwd_body(cur_ref, prev_ref, next_ref, dt_ref, cw_ref, cb_ref, bias_ref, alog_ref, e3_ref,
                  act_ref, yoff_ref, dtq_ref, st_ref, *, n_chunks):
    s = pl.program_id(1)
    c = n_chunks - 1 - s

    @pl.when(s == 0)
    def _():
        st_ref[...] = jnp.zeros_like(st_ref)

    lane_tile = 512
    rows = CHUNK + 2 * HALO
    has_prev = c > 0
    has_next = c < n_chunks - 1
    for t in range(D_XBC // lane_tile):
        sl = slice(t * lane_tile, (t + 1) * lane_tile)
        stack = jnp.concatenate([jnp.where(has_prev, prev_ref[:, sl], 0.0), cur_ref[:, sl],
                                 jnp.where(has_next, next_ref[:, sl], 0.0)], axis=0)
        w = cw_ref[:, sl]
        conv = cb_ref[:, sl] + w[SSM_CONV // 2:SSM_CONV // 2 + 1, :] * stack[HALO:HALO + CHUNK, :]
        for k in range(SSM_CONV):
            shift = SSM_CONV // 2 - k
            if shift != 0:
                tap = pltpu.roll(stack, shift % rows, 0)[HALO:HALO + CHUNK, :]
                conv = conv + w[k:k + 1, :] * tap
        act_ref[:, sl] = _silu(conv)

    dtq = _softplus(dt_ref[...] + bias_ref[...])
    dtq_ref[...] = dtq
    acum, _, _ = _chunk_decay_terms(dtq, alog_ref)
    lane = lax.broadcasted_iota(jnp.int32, (CHUNK, DT_PAD), 1)
    acum = jnp.where((lane >= N_HEADS) & (lane < 2 * N_HEADS), acum, 0.0)
    total = acum[0:1, :]
    eb_x = _expand_heads(jnp.exp(acum), e3_ref)
    wb_x = _expand_heads(dtq * jnp.exp(total - acum), e3_ref)
    dec_x = eb_x[0:1, :]

    for g in range(N_GROUPS):
        ch = slice(g * GROUP_W, (g + 1) * GROUP_W)
        b_g = act_ref[:, D_SSM + g * D_STATE:D_SSM + (g + 1) * D_STATE].astype(BF16)
        c_g = act_ref[:, D_SSM + N_GROUPS * D_STATE + g * D_STATE:
                      D_SSM + N_GROUPS * D_STATE + (g + 1) * D_STATE].astype(BF16)
        st = st_ref[g]
        yoff_ref[:, ch] = jnp.dot(c_g, st.astype(BF16), preferred_element_type=F32) * eb_x[:, ch]
        xw = (act_ref[:, ch] * wb_x[:, ch]).astype(BF16)
        upd = lax.dot_general(b_g, xw, (((0,), (0,)), ((), ())), preferred_element_type=F32)
        st_ref[g] = st * dec_x[:, ch] + upd


def _ssd_bwd(p_main, dt, conv_w, conv_b, bias_q, alog_q, e3b, batch, seq):
    m = p_main.shape[0]
    nc = seq // CHUNK
    rb = CHUNK // HALO
    last_halo = m // HALO - 1

    def cur_map(b, s):
        return (b * nc + (nc - 1 - s), 0)

    def prev_map(b, s):
        return (jnp.maximum((b * nc + (nc - 1 - s)) * rb - 1, 0), 0)

    def next_map(b, s):
        return (jnp.minimum((b * nc + (nc - 1 - s) + 1) * rb, last_halo), 0)

    const = lambda b, s: (0, 0)
    return pl.pallas_call(
        functools.partial(_ssd_bwd_body, n_chunks=nc),
        grid=(batch, nc),
        in_specs=[pl.BlockSpec((CHUNK, D_XBC), cur_map),
                  pl.BlockSpec((HALO, D_XBC), prev_map),
                  pl.BlockSpec((HALO, D_XBC), next_map),
                  pl.BlockSpec((CHUNK, DT_PAD), cur_map),
                  pl.BlockSpec((HALO, D_XBC), const),
                  pl.BlockSpec((1, D_XBC), const),
                  pl.BlockSpec((1, DT_PAD), const),
                  pl.BlockSpec((1, DT_PAD), const),
                  pl.BlockSpec((3 * DT_PAD, D_SSM), const)],
        out_specs=[pl.BlockSpec((CHUNK, D_XBC), cur_map),
                   pl.BlockSpec((CHUNK, D_SSM), cur_map),
                   pl.BlockSpec((CHUNK, DT_PAD), cur_map)],
        out_shape=[jax.ShapeDtypeStruct((m, D_XBC), F32),
                   jax.ShapeDtypeStruct((m, D_SSM), F32),
                   jax.ShapeDtypeStruct((m, DT_PAD), F32)],
        scratch_shapes=[pltpu.VMEM((N_GROUPS, D_STATE, GROUP_W), F32)],
        compiler_params=_cparams(("arbitrary", "arbitrary")),
        name="ssd_bwd",
    )(p_main, p_main, p_main, dt, conv_w, conv_b, bias_q, alog_q, e3b)


def _ssd_fwd_body(act_ref, z_ref, dtq_ref, yoffb_ref, alog_ref, dskip_ref, normw_ref, e3_ref,
                  y_ref, st_ref):
    @pl.when(pl.program_id(1) == 0)
    def _():
        st_ref[...] = jnp.zeros_like(st_ref)

    dtq = dtq_ref[...]
    acum, tril, triu = _chunk_decay_terms(dtq, alog_ref)
    acum_t = acum.T
    dtq_t = dtq.T
    lane = lax.broadcasted_iota(jnp.int32, (CHUNK, DT_PAD), 1)
    acum_f = jnp.where(lane < N_HEADS, acum, 0.0)
    total = acum_f[CHUNK - 1:CHUNK, :]
    ef_x = _expand_heads(jnp.exp(acum_f), e3_ref)
    wf_x = _expand_heads(dtq * jnp.exp(total - acum_f), e3_ref)
    dec_x = ef_x[CHUNK - 1:CHUNK, :]
    head_of_lane = lax.broadcasted_iota(jnp.int32, (CHUNK, GROUP_W), 1) // HEAD_DIM
    neg_inf = jnp.float32(-jnp.inf)

    for g in range(N_GROUPS):
        ch = slice(g * GROUP_W, (g + 1) * GROUP_W)
        b_g = act_ref[:, D_SSM + g * D_STATE:D_SSM + (g + 1) * D_STATE].astype(BF16)
        c_g = act_ref[:, D_SSM + N_GROUPS * D_STATE + g * D_STATE:
                      D_SSM + N_GROUPS * D_STATE + (g + 1) * D_STATE].astype(BF16)
        x_g = act_ref[:, ch]
        scores = lax.dot_general(c_g, b_g, (((1,), (1,)), ((), ())), preferred_element_type=F32)

        mats = []
        blocks = []
        for r in range(HEADS_PER_GROUP):
            hf = g * HEADS_PER_GROUP + r
            hb = N_HEADS + hf
            seg_f = acum[:, hf:hf + 1] - acum_t[hf:hf + 1, :]
            seg_b = acum[:, hb:hb + 1] - acum_t[hb:hb + 1, :]
            lf = jnp.exp(jnp.where(tril, seg_f, neg_inf)) * dtq_t[hf:hf + 1, :]
            ub = jnp.exp(jnp.where(triu, seg_b, neg_inf)) * dtq_t[hb:hb + 1, :]
            mats.append((scores * (lf + ub)).astype(BF16))
            blocks.append(jnp.where(head_of_lane == r, x_g, 0.0).astype(BF16))
        y_diag = jnp.dot(jnp.concatenate(mats, axis=1), jnp.concatenate(blocks, axis=0),
                         preferred_element_type=F32)

        st = st_ref[g]
        y_off = jnp.dot(c_g, st.astype(BF16), preferred_element_type=F32) * ef_x[:, ch]
        y = y_diag + y_off + yoffb_ref[:, ch] + dskip_ref[:, ch] * x_g

        xw = (x_g * wf_x[:, ch]).astype(BF16)
        upd = lax.dot_general(b_g, xw, (((0,), (0,)), ((), ())), preferred_element_type=F32)
        st_ref[g] = st * dec_x[:, ch] + upd

        y = y * _silu(z_ref[:, ch].astype(F32))
        ms = jnp.mean(y * y, axis=-1, keepdims=True)
        y_ref[:, ch] = (y * lax.rsqrt(ms + RMS_EPS) * normw_ref[:, ch]).astype(BF16)


def _ssd_fwd(act, p_rest, dtq, yoffb, alog_q, dskip_x, norm_w, e3f, batch, seq):
    m = act.shape[0]
    nc = seq // CHUNK
    cur = lambda b, s: (b * nc + s, 0)
    const = lambda b, s: (0, 0)
    return pl.pallas_call(
        _ssd_fwd_body,
        grid=(batch, nc),
        in_specs=[pl.BlockSpec((CHUNK, D_XBC), cur),
                  pl.BlockSpec((CHUNK, D_SSM), lambda b, s: (b * nc + s, COL_Z)),
                  pl.BlockSpec((CHUNK, DT_PAD), cur),
                  pl.BlockSpec((CHUNK, D_SSM), cur),
                  pl.BlockSpec((1, DT_PAD), const),
                  pl.BlockSpec((1, D_SSM), const),
                  pl.BlockSpec((1, D_SSM), const),
                  pl.BlockSpec((3 * DT_PAD, D_SSM), const)],
        out_specs=pl.BlockSpec((CHUNK, D_SSM), cur),
        out_shape=jax.ShapeDtypeStruct((m, D_SSM), BF16),
        scratch_shapes=[pltpu.VMEM((N_GROUPS, D_STATE, GROUP_W), F32)],
        compiler_params=_cparams(("arbitrary", "arbitrary")),
        name="ssd_fwd",
    )(act, p_rest, dtq, yoffb, alog_q, dskip_x, norm_w, e3f)


def _shortconv_body(h_ref, b_ref, c_ref, hp_ref, cp_ref, hn_ref, cn_ref, cw_ref, nw_ref, y_ref,
                    *, tiles_per_seq, rows):
    i = pl.program_id(0)
    t = i % tiles_per_seq
    has_prev = t > 0
    has_next = t < tiles_per_seq - 1
    lane_tile = 512
    row = lax.broadcasted_iota(jnp.int32, (rows, lane_tile), 0)
    group = D_SC // SC_GROUPS
    for q in range(D_SC // lane_tile):
        sl = slice(q * lane_tile, (q + 1) * lane_tile)
        v = c_ref[:, sl].astype(F32) * h_ref[:, sl].astype(F32)
        last = HALO_BF16 - 1
        vp = jnp.where(has_prev, cp_ref[:, sl].astype(F32)[last:, :] * hp_ref[:, sl].astype(F32)[last:, :], 0.0)
        vn = jnp.where(has_next, cn_ref[:, sl].astype(F32)[0:1, :] * hn_ref[:, sl].astype(F32)[0:1, :], 0.0)
        down = jnp.where(row == 0, vp, pltpu.roll(v, 1, 0))
        up = jnp.where(row == rows - 1, vn, pltpu.roll(v, rows - 1, 0))
        w = cw_ref[:, sl]
        y = b_ref[:, sl].astype(F32) * (w[0:1, :] * down + w[1:2, :] * v + w[2:3, :] * up)
        for k in range(lane_tile // group):
            gs = slice(k * group, (k + 1) * group)
            yg = y[:, gs]
            ms = jnp.mean(yg * yg, axis=-1, keepdims=True)
            lo = q * lane_tile + k * group
            y_ref[:, lo:lo + group] = (yg * lax.rsqrt(ms + RMS_EPS) * nw_ref[:, lo:lo + group]).astype(BF16)


def _shortconv(p_rest, conv_w, norm_w, seq):
    m = p_rest.shape[0]
    rows = 256
    tiles_per_seq = seq // rows
    rb = rows // HALO_BF16
    last_halo = m // HALO_BF16 - 1

    def cur(col):
        return lambda i: (i, col)

    def prev(col):
        return lambda i: (jnp.maximum(i * rb - 1, 0), col)

    def nxt(col):
        return lambda i: (jnp.minimum((i + 1) * rb, last_halo), col)

    const = lambda i: (0, 0)
    return pl.pallas_call(
        functools.partial(_shortconv_body, tiles_per_seq=tiles_per_seq, rows=rows),
        grid=(m // rows,),
        in_specs=[pl.BlockSpec((rows, D_SC), cur(COL_H)),
                  pl.BlockSpec((rows, D_SC), cur(COL_B)),
                  pl.BlockSpec((rows, D_SC), cur(COL_C)),
                  pl.BlockSpec((HALO_BF16, D_SC), prev(COL_H)),
                  pl.BlockSpec((HALO_BF16, D_SC), prev(COL_C)),
                  pl.BlockSpec((HALO_BF16, D_SC), nxt(COL_H)),
                  pl.BlockSpec((HALO_BF16, D_SC), nxt(COL_C)),
                  pl.BlockSpec((HALO, D_SC), const),
                  pl.BlockSpec((1, D_SC), const)],
        out_specs=pl.BlockSpec((rows, D_SC), lambda i: (i, 0)),
        out_shape=jax.ShapeDtypeStruct((m, D_SC), BF16),
        compiler_params=_cparams(("arbitrary",)),
        name="shortconv",
    )(p_rest, p_rest, p_rest, p_rest, p_rest, p_rest, p_rest, conv_w, norm_w)


def _residual_layer_norm(o_ref, x_ref, gate_ref, lng_ref, lnb_ref, alpha, modulated=None):
    width = o_ref.shape[1]
    lane_tiles = [slice(t, t + LANES) for t in range(0, width, LANES)]

    def slab(i, carry):
        sl = pl.ds(pl.multiple_of(i * LN_ROWS, LN_ROWS), LN_ROWS)
        acc = jnp.zeros((LN_ROWS, LANES), F32)
        for lt in lane_tiles:
            r = alpha * x_ref[sl, lt] + (1.0 + gate_ref[0, :, lt]) * o_ref[sl, lt]
            o_ref[sl, lt] = r
            acc = acc + r
        mean = jnp.broadcast_to(jnp.sum(acc, axis=-1, keepdims=True) * (1.0 / width), (LN_ROWS, LANES))
        acc = jnp.zeros((LN_ROWS, LANES), F32)
        for lt in lane_tiles:
            d = o_ref[sl, lt] - mean
            acc = acc + d * d
        var = jnp.sum(acc, axis=-1, keepdims=True) * (1.0 / width)
        rstd = jnp.broadcast_to(lax.rsqrt(var + LN_EPS), (LN_ROWS, LANES))
        for lt in lane_tiles:
            y = (o_ref[sl, lt] - mean) * rstd * lng_ref[:, lt] + lnb_ref[:, lt]
            o_ref[sl, lt] = y
            if modulated is not None:
                h_ref, sc_ref, sh_ref = modulated
                h_ref[sl, lt] = (y * (1.0 + sc_ref[0, :, lt]) + sh_ref[0, :, lt]).astype(BF16)
        return carry

    lax.fori_loop(0, o_ref.shape[0] // LN_ROWS, slab, 0)


def _row_tile_copy(hbm_ref, buf_ref, sem, tm):
    rows = pl.ds(pl.multiple_of(pl.program_id(0) * tm, tm), tm)
    return pltpu.make_async_copy(hbm_ref.at[rows, :], buf_ref, sem)


def _outproj_body(ys_ref, yc_ref, w_ref, x_hbm, gate_ref, sc_ref, sh_ref, lng_ref, lnb_ref, x1_ref, h2_ref,
                  x_buf, sem, *, alpha, tm, tn):
    n = pl.program_id(1)

    @pl.when(n == 0)
    def _():
        _row_tile_copy(x_hbm, x_buf, sem, tm).start()

    k_ssm = ys_ref.shape[1]
    cols = pl.ds(pl.multiple_of(n * tn, tn), tn)
    x1_ref[:, cols] = (jnp.dot(ys_ref[...], w_ref[:k_ssm, :], preferred_element_type=F32)
                       + jnp.dot(yc_ref[...], w_ref[k_ssm:, :], preferred_element_type=F32))

    @pl.when(n == pl.num_programs(1) - 1)
    def _():
        _row_tile_copy(x_hbm, x_buf, sem, tm).wait()
        _residual_layer_norm(x1_ref, x_buf, gate_ref, lng_ref, lnb_ref, alpha,
                             modulated=(h2_ref, sc_ref, sh_ref))


def _outproj(y_ssm, y_sc, w_out, x2, gate, scale2, shift2, ln_g, ln_b, seq, alpha):
    m, d = x2.shape
    tm, tn = 512, 1024
    tiles_per_batch = seq // tm
    mod_map = lambda i, n: (i // tiles_per_batch, 0, 0)
    const = lambda i, n: (0, 0)
    row_tile = lambda i, n: (i, 0)
    return pl.pallas_call(
        functools.partial(_outproj_body, alpha=alpha, tm=tm, tn=tn),
        grid=(m // tm, d // tn),
        in_specs=[pl.BlockSpec((tm, D_SSM), row_tile),
                  pl.BlockSpec((tm, D_SC), row_tile),
                  pl.BlockSpec((D_SSM + D_SC, tn), lambda i, n: (0, n)),
                  pl.BlockSpec(memory_space=pl.ANY),
                  pl.BlockSpec((1, 1, d), mod_map),
                  pl.BlockSpec((1, 1, d), mod_map),
                  pl.BlockSpec((1, 1, d), mod_map),
                  pl.BlockSpec((1, d), const),
                  pl.BlockSpec((1, d), const)],
        out_specs=[pl.BlockSpec((tm, d), row_tile),
                   pl.BlockSpec((tm, d), row_tile)],
        out_shape=[jax.ShapeDtypeStruct((m, d), F32),
                   jax.ShapeDtypeStruct((m, d), BF16)],
        scratch_shapes=[pltpu.VMEM((tm, d), F32), pltpu.SemaphoreType.DMA(())],
        compiler_params=_cparams(("arbitrary", "arbitrary")),
        name="outproj",
    )(y_ssm, y_sc, w_out, x2, gate, scale2, shift2, ln_g, ln_b)


def _ffn_body(h2_ref, x1_hbm, gate_ref, wu_ref, wd_ref, lng_ref, lnb_ref, o_ref, x1_buf, sem, *, alpha, tm):
    f = pl.program_id(1)

    @pl.when(f == 0)
    def _():
        _row_tile_copy(x1_hbm, x1_buf, sem, tm).start()
        o_ref[...] = jnp.zeros_like(o_ref)

    u = jnp.maximum(jnp.dot(h2_ref[...], wu_ref[...], preferred_element_type=F32), 0.0)
    u = (u * u).astype(BF16)
    for n0 in range(0, o_ref.shape[1], ACC_COLS):
        cols = slice(n0, n0 + ACC_COLS)
        o_ref[:, cols] += jnp.dot(u, wd_ref[:, cols], preferred_element_type=F32)

    @pl.when(f == pl.num_programs(1) - 1)
    def _():
        _row_tile_copy(x1_hbm, x1_buf, sem, tm).wait()
        _residual_layer_norm(o_ref, x1_buf, gate_ref, lng_ref, lnb_ref, alpha)


def _ffn(h2, x1, gate, w_up, w_down, ln_g, ln_b, seq, alpha):
    m, d = x1.shape
    ff = w_up.shape[1]
    tm, tf = 512, 1024
    tiles_per_batch = seq // tm
    mod_map = lambda i, f: (i // tiles_per_batch, 0, 0)
    const = lambda i, f: (0, 0)
    row_tile = lambda i, f: (i, 0)
    return pl.pallas_call(
        functools.partial(_ffn_body, alpha=alpha, tm=tm),
        grid=(m // tm, ff // tf),
        in_specs=[pl.BlockSpec((tm, d), row_tile),
                  pl.BlockSpec(memory_space=pl.ANY),
                  pl.BlockSpec((1, 1, d), mod_map),
                  pl.BlockSpec((d, tf), lambda i, f: (0, f)),
                  pl.BlockSpec((tf, d), lambda i, f: (f, 0)),
                  pl.BlockSpec((1, d), const),
                  pl.BlockSpec((1, d), const)],
        out_specs=pl.BlockSpec((tm, d), row_tile, pipeline_mode=pl.Buffered(1)),
        out_shape=jax.ShapeDtypeStruct((m, d), F32),
        scratch_shapes=[pltpu.VMEM((tm, d), F32), pltpu.SemaphoreType.DMA(())],
        compiler_params=_cparams(("arbitrary", "arbitrary")),
        name="ffn",
    )(h2, x1, gate, w_up, w_down, ln_g, ln_b)


def _head_expansion(lane_offset):
    rows = jnp.arange(DT_PAD)[:, None]
    heads = jnp.arange(D_SSM)[None, :] // HEAD_DIM
    e = (rows == heads + lane_offset).astype(BF16)
    return jnp.tile(e, (3, 1))


def _pad_rows(a, rows):
    return jnp.pad(a, ((0, rows - a.shape[0]), (0, 0)))


def kernel(x, c, w_ada, b_ada, w_in, ssm_conv_w, ssm_conv_b, ssm_dt_bias_f, ssm_dt_bias_b, ssm_a_log_f, ssm_a_log_b, ssm_d, ssm_norm_w, sc_conv_w, sc_norm_w, w_out, ln1_g, ln1_b, w_up, w_down, ln2_g, ln2_b):
    batch, seq, d = x.shape
    depth = w_ada.shape[0]
    alpha = float((2 * depth) ** 0.25)
    m = batch * seq
    dt_lo = D_SSM + D_XBC
    dt_hi = dt_lo + 2 * N_HEADS
    e3f = _head_expansion(0)
    e3b = _head_expansion(N_HEADS)
    c_pad = _pad_rows(c, HALO)
    x2 = x.reshape(m, d)

    for l in range(depth):
        mod = _ada(c_pad, w_ada[l], b_ada[l][None, :])[:batch]
        shift1, scale1, gate1, shift2, scale2, gate2 = [
            t[:, None, :] for t in jnp.split(mod, N_MOD, axis=-1)]

        w_t = w_in[l].T.astype(BF16)
        w_dt_t = jnp.pad(w_t[dt_lo:dt_hi], ((0, DT_PAD - 2 * N_HEADS), (0, 0)))
        p_xbc, p_rest, dt = _inproj(x2, scale1, shift1, w_t, w_dt_t, seq)

        zeros_q = jnp.zeros((DT_PAD - 2 * N_HEADS,), F32)
        bias_q = jnp.concatenate([ssm_dt_bias_f[l], ssm_dt_bias_b[l], zeros_q])[None, :]
        alog_q = jnp.concatenate([ssm_a_log_f[l], ssm_a_log_b[l], zeros_q])[None, :]
        act, yoffb, dtq = _ssd_bwd(p_xbc, dt, _pad_rows(ssm_conv_w[l], HALO), ssm_conv_b[l][None, :],
                                   bias_q, alog_q, e3b, batch, seq)
        dskip_x = jnp.repeat(ssm_d[l], HEAD_DIM)[None, :]
        y_ssm = _ssd_fwd(act, p_rest, dtq, yoffb, alog_q, dskip_x, ssm_norm_w[l][None, :], e3f, batch, seq)
        y_sc = _shortconv(p_rest, _pad_rows(sc_conv_w[l], HALO), sc_norm_w[l][None, :], seq)

        x1, h2 = _outproj(y_ssm, y_sc, w_out[l].astype(BF16), x2, gate1, scale2, shift2,
                          ln1_g[l][None, :], ln1_b[l][None, :], seq, alpha)
        x2 = _ffn(h2, x1, gate2, w_up[l].astype(BF16), w_down[l].astype(BF16),
                  ln2_g[l][None, :], ln2_b[l][None, :], seq, alpha)
    return x2.reshape(batch, seq, d)
```

```python
import functools

import jax
import jax.numpy as jnp
from jax import lax
from jax.experimental import pallas as pl
from jax.experimental.pallas import tpu as pltpu

F32 = jnp.float32
BF16 = jnp.bfloat16
HIGHEST = lax.Precision.HIGHEST

D_MODEL = 4096
D_SSM = 2048
D_SC = 2048
HEAD_DIM = 64
N_HEADS = 32
N_GROUPS = 8
HEADS_PER_GROUP = 4
D_STATE = 128
CHUNK = 128
SSM_CONV = 5
SC_CONV = 3
SC_GROUPS = 16
D_XBC = D_SSM + 2 * N_GROUPS * D_STATE
D_FF = 4 * D_MODEL
N_MOD = 6
LN_EPS = 1e-5
RMS_EPS = 1e-5
GROUP_W = HEADS_PER_GROUP * HEAD_DIM
D_MAIN = D_XBC + D_SSM + 3 * D_SC
DT_PAD = 128
HALO = 8
HALO_BF16 = 16
LANES = 128
LN_ROWS = 64
ACC_COLS = 512

COL_Z = 0
COL_H = 1
COL_B = 2
COL_C = 3

VMEM_LIMIT = 62 * 1024 * 1024


def _cparams(sem):
    return pltpu.CompilerParams(dimension_semantics=sem, vmem_limit_bytes=VMEM_LIMIT)


def _sigmoid(v):
    return 1.0 / (1.0 + jnp.exp(-v))


def _silu(v):
    return v * _sigmoid(v)


def _softplus(v):
    return jnp.maximum(v, 0.0) + jnp.log1p(jnp.exp(-jnp.abs(v)))


def _layer_norm(r, g, b):
    mu = jnp.mean(r, axis=-1, keepdims=True)
    d = r - mu
    var = jnp.mean(d * d, axis=-1, keepdims=True)
    return d * lax.rsqrt(var + LN_EPS) * g + b


def _ada_body(c_ref, w_ref, b_ref, o_ref):
    o_ref[...] = jnp.dot(_silu(c_ref[...]).astype(BF16), w_ref[...].astype(BF16),
                         preferred_element_type=F32) + b_ref[...]


def _ada(c_pad, w_ada, b_ada):
    rows, d = c_pad.shape
    n = w_ada.shape[1]
    tn = 512
    return pl.pallas_call(
        _ada_body,
        grid=(n // tn,),
        in_specs=[pl.BlockSpec((rows, d), lambda j: (0, 0)),
                  pl.BlockSpec((d, tn), lambda j: (0, j)),
                  pl.BlockSpec((1, tn), lambda j: (0, j))],
        out_specs=pl.BlockSpec((rows, tn), lambda j: (0, j)),
        out_shape=jax.ShapeDtypeStruct((rows, n), F32),
        compiler_params=_cparams(("arbitrary",)),
        name="ada",
    )(c_pad, w_ada, b_ada)


def _dot_nt(a, b_t):
    return lax.dot_general(a, b_t, (((1,), (1,)), ((), ())), preferred_element_type=F32)


def _inproj_body(x_ref, sc_ref, sh_ref, w_ref, wdt_ref, xbc_ref, rest_ref, dt_ref, h_scr, *, xbc_tiles):
    j = pl.program_id(1)

    @pl.when(j == 0)
    def _():
        h = (x_ref[...] * (1.0 + sc_ref[0]) + sh_ref[0]).astype(BF16)
        h_scr[...] = h
        dt_ref[...] = _dot_nt(h, wdt_ref[...])

    @pl.when(j < xbc_tiles)
    def _():
        xbc_ref[...] = _dot_nt(h_scr[...], w_ref[...])

    @pl.when(j >= xbc_tiles)
    def _():
        rest_ref[...] = _dot_nt(h_scr[...], w_ref[...]).astype(BF16)


def _inproj(x2, scale, shift, w_t, w_dt_t, seq):
    m, d = x2.shape
    n = D_MAIN
    tm, tn = 1024, 512
    tiles_per_batch = seq // tm
    xbc_tiles = D_XBC // tn
    z_tiles = D_SSM // tn
    hbc_start = D_SSM + D_XBC + 2 * N_HEADS

    def w_row_start(j):
        start = jnp.where(j < xbc_tiles, D_SSM + j * tn,
                          jnp.where(j < xbc_tiles + z_tiles, (j - xbc_tiles) * tn,
                                    hbc_start + (j - xbc_tiles - z_tiles) * tn))
        return pl.multiple_of(start, 2 * N_HEADS)
    return pl.pallas_call(
        functools.partial(_inproj_body, xbc_tiles=xbc_tiles),
        grid=(m // tm, n // tn),
        in_specs=[pl.BlockSpec((tm, d), lambda i, j: (i, 0)),
                  pl.BlockSpec((1, 1, d), lambda i, j: (i // tiles_per_batch, 0, 0)),
                  pl.BlockSpec((1, 1, d), lambda i, j: (i // tiles_per_batch, 0, 0)),
                  pl.BlockSpec((pl.Element(tn), pl.Element(d)), lambda i, j: (w_row_start(j), 0)),
                  pl.BlockSpec((DT_PAD, d), lambda i, j: (0, 0))],
        out_specs=[pl.BlockSpec((tm, tn), lambda i, j: (i, jnp.minimum(j, xbc_tiles - 1))),
                   pl.BlockSpec((tm, tn), lambda i, j: (i, jnp.maximum(j - xbc_tiles, 0))),
                   pl.BlockSpec((tm, DT_PAD), lambda i, j: (i, 0))],
        out_shape=[jax.ShapeDtypeStruct((m, D_XBC), F32),
                   jax.ShapeDtypeStruct((m, n - D_XBC), BF16),
                   jax.ShapeDtypeStruct((m, DT_PAD), F32)],
        scratch_shapes=[pltpu.VMEM((tm, d), BF16)],
        compiler_params=_cparams(("arbitrary", "arbitrary")),
        name="inproj",
    )(x2, scale, shift, w_t, w_dt_t)


def _expand_heads(q, e3_ref):
    hi = q.astype(BF16)
    r1 = q - hi.astype(F32)
    mid = r1.astype(BF16)
    lo = (r1 - mid.astype(F32)).astype(BF16)
    pieces = jnp.concatenate([hi, mid, lo], axis=1)
    return jnp.dot(pieces, e3_ref[...], preferred_element_type=F32)


def _chunk_decay_terms(dtq, alog_ref):
    row = lax.broadcasted_iota(jnp.int32, (CHUNK, CHUNK), 0)
    col = lax.broadcasted_iota(jnp.int32, (CHUNK, CHUNK), 1)
    tril = col <= row
    triu = col >= row
    da = dtq * (-jnp.exp(alog_ref[...]))
    pre = jnp.dot(tril.astype(F32), da, preferred_element_type=F32, precision=HIGHEST)
    suf = jnp.dot(triu.astype(F32), da, preferred_element_type=F32, precision=HIGHEST)
    acum = jnp.where(col < N_HEADS, pre, suf)
    return acum, tril, triu


def _ssd_bwd_body(cur_ref, prev_ref, next_ref, dt_ref, cw_ref, cb_ref, bias_ref, alog_ref, e3_ref,
                  act_ref, yoff_ref, dtq_ref, st_ref, *, n_chunks):
    s = pl.program_id(1)
    c = n_chunks - 1 - s

    @pl.when(s == 0)
    def _():
        st_ref[...] = jnp.zeros_like(st_ref)

    lane_tile = 512
    rows = CHUNK + 2 * HALO
    has_prev = c > 0
    has_next = c < n_chunks - 1
    for t in range(D_XBC // lane_tile):
        sl = slice(t * lane_tile, (t + 1) * lane_tile)
        stack = jnp.concatenate([jnp.where(has_prev, prev_ref[:, sl], 0.0), cur_ref[:, sl],
                                 jnp.where(has_next, next_ref[:, sl], 0.0)], axis=0)
        w = cw_ref[:, sl]
        conv = cb_ref[:, sl] + w[SSM_CONV // 2:SSM_CONV // 2 + 1, :] * stack[HALO:HALO + CHUNK, :]
        for k in range(SSM_CONV):
            shift = SSM_CONV // 2 - k
            if shift != 0:
                tap = pltpu.roll(stack, shift % rows, 0)[HALO:HALO + CHUNK, :]
                conv = conv + w[k:k + 1, :] * tap
        act_ref[:, sl] = _silu(conv)

    dtq = _softplus(dt_ref[...] + bias_ref[...])
    dtq_ref[...] = dtq
    acum, _, _ = _chunk_decay_terms(dtq, alog_ref)
    lane = lax.broadcasted_iota(jnp.int32, (CHUNK, DT_PAD), 1)
    acum = jnp.where((lane >= N_HEADS) & (lane < 2 * N_HEADS), acum, 0.0)
    total = acum[0:1, :]
    eb_x = _expand_heads(jnp.exp(acum), e3_ref)
    wb_x = _expand_heads(dtq * jnp.exp(total - acum), e3_ref)
    dec_x = eb_x[0:1, :]

    for g in range(N_GROUPS):
        ch = slice(g * GROUP_W, (g + 1) * GROUP_W)
        b_g = act_ref[:, D_SSM + g * D_STATE:D_SSM + (g + 1) * D_STATE].astype(BF16)
        c_g = act_ref[:, D_SSM + N_GROUPS * D_STATE + g * D_STATE:
                      D_SSM + N_GROUPS * D_STATE + (g + 1) * D_STATE].astype(BF16)
        st = st_ref[g]
        yoff_ref[:, ch] = jnp.dot(c_g, st.astype(BF16), preferred_element_type=F32) * eb_x[:, ch]
        xw = (act_ref[:, ch] * wb_x[:, ch]).astype(BF16)
        upd = lax.dot_general(b_g, xw, (((0,), (0,)), ((), ())), preferred_element_type=F32)
        st_ref[g] = st * dec_x[:, ch] + upd


def _ssd_bwd(p_main, dt, conv_w, conv_b, bias_q, alog_q, e3b, batch, seq):
    m = p_main.shape[0]
    nc = seq // CHUNK
    rb = CHUNK // HALO
    last_halo = m // HALO - 1

    def cur_map(b, s):
        return (b * nc + (nc - 1 - s), 0)

    def prev_map(b, s):
        return (jnp.maximum((b * nc + (nc - 1 - s)) * rb - 1, 0), 0)

    def next_map(b, s):
        return (jnp.minimum((b * nc + (nc - 1 - s) + 1) * rb, last_halo), 0)

    const = lambda b, s: (0, 0)
    return pl.pallas_call(
        functools.partial(_ssd_bwd_body, n_chunks=nc),
        grid=(batch, nc),
        in_specs=[pl.BlockSpec((CHUNK, D_XBC), cur_map),
                  pl.BlockSpec((HALO, D_XBC), prev_map),
                  pl.BlockSpec((HALO, D_XBC), next_map),
                  pl.BlockSpec((CHUNK, DT_PAD), cur_map),
                  pl.BlockSpec((HALO, D_XBC), const),
                  pl.BlockSpec((1, D_XBC), const),
                  pl.BlockSpec((1, DT_PAD), const),
                  pl.BlockSpec((1, DT_PAD), const),
                  pl.BlockSpec((3 * DT_PAD, D_SSM), const)],
        out_specs=[pl.BlockSpec((CHUNK, D_XBC), cur_map),
                   pl.BlockSpec((CHUNK, D_SSM), cur_map),
                   pl.BlockSpec((CHUNK, DT_PAD), cur_map)],
        out_shape=[jax.ShapeDtypeStruct((m, D_XBC), F32),
                   jax.ShapeDtypeStruct((m, D_SSM), F32),
                   jax.ShapeDtypeStruct((m, DT_PAD), F32)],
        scratch_shapes=[pltpu.VMEM((N_GROUPS, D_STATE, GROUP_W), F32)],
        compiler_params=_cparams(("arbitrary", "arbitrary")),
        name="ssd_bwd",
    )(p_main, p_main, p_main, dt, conv_w, conv_b, bias_q, alog_q, e3b)


def _ssd_fwd_body(act_ref, z_ref, dtq_ref, yoffb_ref, alog_ref, dskip_ref, normw_ref, e3_ref,
                  y_ref, st_ref):
    @pl.when(pl.program_id(1) == 0)
    def _():
        st_ref[...] = jnp.zeros_like(st_ref)

    dtq = dtq_ref[...]
    acum, tril, triu = _chunk_decay_terms(dtq, alog_ref)
    acum_t = acum.T
    dtq_t = dtq.T
    lane = lax.broadcasted_iota(jnp.int32, (CHUNK, DT_PAD), 1)
    acum_f = jnp.where(lane < N_HEADS, acum, 0.0)
    total = acum_f[CHUNK - 1:CHUNK, :]
    ef_x = _expand_heads(jnp.exp(acum_f), e3_ref)
    wf_x = _expand_heads(dtq * jnp.exp(total - acum_f), e3_ref)
    dec_x = ef_x[CHUNK - 1:CHUNK, :]
    head_of_lane = lax.broadcasted_iota(jnp.int32, (CHUNK, GROUP_W), 1) // HEAD_DIM
    neg_inf = jnp.float32(-jnp.inf)

    for g in range(N_GROUPS):
        ch = slice(g * GROUP_W, (g + 1) * GROUP_W)
        b_g = act_ref[:, D_SSM + g * D_STATE:D_SSM + (g + 1) * D_STATE].astype(BF16)
        c_g = act_ref[:, D_SSM + N_GROUPS * D_STATE + g * D_STATE:
                      D_SSM + N_GROUPS * D_STATE + (g + 1) * D_STATE].astype(BF16)
        x_g = act_ref[:, ch]
        scores = lax.dot_general(c_g, b_g, (((1,), (1,)), ((), ())), preferred_element_type=F32)

        mats = []
        blocks = []
        for r in range(HEADS_PER_GROUP):
            hf = g * HEADS_PER_GROUP + r
            hb = N_HEADS + hf
            seg_f = acum[:, hf:hf + 1] - acum_t[hf:hf + 1, :]
            seg_b = acum[:, hb:hb + 1] - acum_t[hb:hb + 1, :]
            lf = jnp.exp(jnp.where(tril, seg_f, neg_inf)) * dtq_t[hf:hf + 1, :]
            ub = jnp.exp(jnp.where(triu, seg_b, neg_inf)) * dtq_t[hb:hb + 1, :]
            mats.append((scores * (lf + ub)).astype(BF16))
            blocks.append(jnp.where(head_of_lane == r, x_g, 0.0).astype(BF16))
        y_diag = jnp.dot(jnp.concatenate(mats, axis=1), jnp.concatenate(blocks, axis=0),
                         preferred_element_type=F32)

        st = st_ref[g]
        y_off = jnp.dot(c_g, st.astype(BF16), preferred_element_type=F32) * ef_x[:, ch]
        y = y_diag + y_off + yoffb_ref[:, ch] + dskip_ref[:, ch] * x_g

        xw = (x_g * wf_x[:, ch]).astype(BF16)
        upd = lax.dot_general(b_g, xw, (((0,), (0,)), ((), ())), preferred_element_type=F32)
        st_ref[g] = st * dec_x[:, ch] + upd

        y = y * _silu(z_ref[:, ch].astype(F32))
        ms = jnp.mean(y * y, axis=-1, keepdims=True)
        y_ref[:, ch] = (y * lax.rsqrt(ms + RMS_EPS) * normw_ref[:, ch]).astype(BF16)


def _ssd_fwd(act, p_rest, dtq, yoffb, alog_q, dskip_x, norm_w, e3f, batch, seq):
    m = act.shape[0]
    nc = seq // CHUNK
    cur = lambda b, s: (b * nc + s, 0)
    const = lambda b, s: (0, 0)
    return pl.pallas_call(
        _ssd_fwd_body,
        grid=(batch, nc),
        in_specs=[pl.BlockSpec((CHUNK, D_XBC), cur),
                  pl.BlockSpec((CHUNK, D_SSM), lambda b, s: (b * nc + s, COL_Z)),
                  pl.BlockSpec((CHUNK, DT_PAD), cur),
                  pl.BlockSpec((CHUNK, D_SSM), cur),
                  pl.BlockSpec((1, DT_PAD), const),
                  pl.BlockSpec((1, D_SSM), const),
                  pl.BlockSpec((1, D_SSM), const),
                  pl.BlockSpec((3 * DT_PAD, D_SSM), const)],
        out_specs=pl.BlockSpec((CHUNK, D_SSM), cur),
        out_shape=jax.ShapeDtypeStruct((m, D_SSM), BF16),
        scratch_shapes=[pltpu.VMEM((N_GROUPS, D_STATE, GROUP_W), F32)],
        compiler_params=_cparams(("arbitrary", "arbitrary")),
        name="ssd_fwd",
    )(act, p_rest, dtq, yoffb, alog_q, dskip_x, norm_w, e3f)


def _shortconv_body(h_ref, b_ref, c_ref, hp_ref, cp_ref, hn_ref, cn_ref, cw_ref, nw_ref, y_ref,
                    *, tiles_per_seq, rows):
    i = pl.program_id(0)
    t = i % tiles_per_seq
    has_prev = t > 0
    has_next = t < tiles_per_seq - 1
    lane_tile = 512
    row = lax.broadcasted_iota(jnp.int32, (rows, lane_tile), 0)
    group = D_SC // SC_GROUPS
    for q in range(D_SC // lane_tile):
        sl = slice(q * lane_tile, (q + 1) * lane_tile)
        v = c_ref[:, sl].astype(F32) * h_ref[:, sl].astype(F32)
        last = HALO_BF16 - 1
        vp = jnp.where(has_prev, cp_ref[:, sl].astype(F32)[last:, :] * hp_ref[:, sl].astype(F32)[last:, :], 0.0)
        vn = jnp.where(has_next, cn_ref[:, sl].astype(F32)[0:1, :] * hn_ref[:, sl].astype(F32)[0:1, :], 0.0)
        down = jnp.where(row == 0, vp, pltpu.roll(v, 1, 0))
        up = jnp.where(row == rows - 1, vn, pltpu.roll(v, rows - 1, 0))
        w = cw_ref[:, sl]
        y = b_ref[:, sl].astype(F32) * (w[0:1, :] * down + w[1:2, :] * v + w[2:3, :] * up)
        for k in range(lane_tile // group):
            gs = slice(k * group, (k + 1) * group)
            yg = y[:, gs]
            ms = jnp.mean(yg * yg, axis=-1, keepdims=True)
            lo = q * lane_tile + k * group
            y_ref[:, lo:lo + group] = (yg * lax.rsqrt(ms + RMS_EPS) * nw_ref[:, lo:lo + group]).astype(BF16)


def _shortconv(p_rest, conv_w, norm_w, seq):
    m = p_rest.shape[0]
    rows = 256
    tiles_per_seq = seq // rows
    rb = rows // HALO_BF16
    last_halo = m // HALO_BF16 - 1

    def cur(col):
        return lambda i: (i, col)

    def prev(col):
        return lambda i: (jnp.maximum(i * rb - 1, 0), col)

    def nxt(col):
        return lambda i: (jnp.minimum((i + 1) * rb, last_halo), col)

    const = lambda i: (0, 0)
    return pl.pallas_call(
        functools.partial(_shortconv_body, tiles_per_seq=tiles_per_seq, rows=rows),
        grid=(m // rows,),
        in_specs=[pl.BlockSpec((rows, D_SC), cur(COL_H)),
                  pl.BlockSpec((rows, D_SC), cur(COL_B)),
                  pl.BlockSpec((rows, D_SC), cur(COL_C)),
                  pl.BlockSpec((HALO_BF16, D_SC), prev(COL_H)),
                  pl.BlockSpec((HALO_BF16, D_SC), prev(COL_C)),
                  pl.BlockSpec((HALO_BF16, D_SC), nxt(COL_H)),
                  pl.BlockSpec((HALO_BF16, D_SC), nxt(COL_C)),
                  pl.BlockSpec((HALO, D_SC), const),
                  pl.BlockSpec((1, D_SC), const)],
        out_specs=pl.BlockSpec((rows, D_SC), lambda i: (i, 0)),
        out_shape=jax.ShapeDtypeStruct((m, D_SC), BF16),
        compiler_params=_cparams(("arbitrary",)),
        name="shortconv",
    )(p_rest, p_rest, p_rest, p_rest, p_rest, p_rest, p_rest, conv_w, norm_w)


def _residual_layer_norm(o_ref, x_ref, gate_ref, lng_ref, lnb_ref, alpha, modulated=None, on_slab=None):
    width = o_ref.shape[1]
    lane_tiles = [slice(t, t + LANES) for t in range(0, width, LANES)]

    def slab(i, carry):
        sl = pl.ds(pl.multiple_of(i * LN_ROWS, LN_ROWS), LN_ROWS)
        acc = jnp.zeros((LN_ROWS, LANES), F32)
        for lt in lane_tiles:
            r = alpha * x_ref[sl, lt] + (1.0 + gate_ref[0, :, lt]) * o_ref[sl, lt]
            o_ref[sl, lt] = r
            acc = acc + r
        mean = jnp.broadcast_to(jnp.sum(acc, axis=-1, keepdims=True) * (1.0 / width), (LN_ROWS, LANES))
        acc = jnp.zeros((LN_ROWS, LANES), F32)
        for lt in lane_tiles:
            d = o_ref[sl, lt] - mean
            acc = acc + d * d
        var = jnp.sum(acc, axis=-1, keepdims=True) * (1.0 / width)
        rstd = jnp.broadcast_to(lax.rsqrt(var + LN_EPS), (LN_ROWS, LANES))
        for lt in lane_tiles:
            y = (o_ref[sl, lt] - mean) * rstd * lng_ref[:, lt] + lnb_ref[:, lt]
            o_ref[sl, lt] = y
            if modulated is not None:
                h_ref, sc_ref, sh_ref = modulated
                h_ref[sl, lt] = (y * (1.0 + sc_ref[0, :, lt]) + sh_ref[0, :, lt]).astype(BF16)
        if on_slab is not None:
            on_slab(i)
        return carry

    lax.fori_loop(0, o_ref.shape[0] // LN_ROWS, slab, 0)


def _row_tile_copy(hbm_ref, buf_ref, sem, tm):
    rows = pl.ds(pl.multiple_of(pl.program_id(0) * tm, tm), tm)
    return pltpu.make_async_copy(hbm_ref.at[rows, :], buf_ref, sem)


def _outproj_body(ys_ref, yc_ref, w_ref, x_hbm, gate_ref, sc_ref, sh_ref, lng_ref, lnb_ref, x1_ref, h2_ref,
                  x_buf, sem, *, alpha, tm, tn):
    n = pl.program_id(1)

    @pl.when(n == 0)
    def _():
        _row_tile_copy(x_hbm, x_buf, sem, tm).start()

    k_ssm = ys_ref.shape[1]
    cols = pl.ds(pl.multiple_of(n * tn, tn), tn)
    x1_ref[:, cols] = (jnp.dot(ys_ref[...], w_ref[:k_ssm, :], preferred_element_type=F32)
                       + jnp.dot(yc_ref[...], w_ref[k_ssm:, :], preferred_element_type=F32))

    @pl.when(n == pl.num_programs(1) - 1)
    def _():
        _row_tile_copy(x_hbm, x_buf, sem, tm).wait()
        _residual_layer_norm(x1_ref, x_buf, gate_ref, lng_ref, lnb_ref, alpha,
                             modulated=(h2_ref, sc_ref, sh_ref))


def _outproj(y_ssm, y_sc, w_out, x2, gate, scale2, shift2, ln_g, ln_b, seq, alpha):
    m, d = x2.shape
    tm, tn = 512, 1024
    tiles_per_batch = seq // tm
    mod_map = lambda i, n: (i // tiles_per_batch, 0, 0)
    const = lambda i, n: (0, 0)
    row_tile = lambda i, n: (i, 0)
    return pl.pallas_call(
        functools.partial(_outproj_body, alpha=alpha, tm=tm, tn=tn),
        grid=(m // tm, d // tn),
        in_specs=[pl.BlockSpec((tm, D_SSM), row_tile),
                  pl.BlockSpec((tm, D_SC), row_tile),
                  pl.BlockSpec((D_SSM + D_SC, tn), lambda i, n: (0, n)),
                  pl.BlockSpec(memory_space=pl.ANY),
                  pl.BlockSpec((1, 1, d), mod_map),
                  pl.BlockSpec((1, 1, d), mod_map),
                  pl.BlockSpec((1, 1, d), mod_map),
                  pl.BlockSpec((1, d), const),
                  pl.BlockSpec((1, d), const)],
        out_specs=[pl.BlockSpec((tm, d), row_tile),
                   pl.BlockSpec((tm, d), row_tile)],
        out_shape=[jax.ShapeDtypeStruct((m, d), F32),
                   jax.ShapeDtypeStruct((m, d), BF16)],
        scratch_shapes=[pltpu.VMEM((tm, d), F32), pltpu.SemaphoreType.DMA(())],
        compiler_params=_cparams(("arbitrary", "arbitrary")),
        name="outproj",
    )(y_ssm, y_sc, w_out, x2, gate, scale2, shift2, ln_g, ln_b)


def _slab_writeback(acc_ref, out_hbm, sems, tm, i):
    src = pl.ds(pl.multiple_of(i * LN_ROWS, LN_ROWS), LN_ROWS)
    dst = pl.ds(pl.multiple_of(pl.program_id(0) * tm + i * LN_ROWS, LN_ROWS), LN_ROWS)
    return pltpu.make_async_copy(acc_ref.at[src, :], out_hbm.at[dst, :], sems.at[i])


def _ffn_body(h2_ref, x1_hbm, gate_ref, wu_ref, wd_ref, lng_ref, lnb_ref, out_hbm, acc_ref, x1_buf, sem,
              out_sems, *, alpha, tm):
    f = pl.program_id(1)

    @pl.when(f == 0)
    def _():
        _row_tile_copy(x1_hbm, x1_buf, sem, tm).start()
        acc_ref[...] = jnp.zeros_like(acc_ref)

    u = jnp.maximum(jnp.dot(h2_ref[...], wu_ref[...], preferred_element_type=F32), 0.0)
    u = (u * u).astype(BF16)
    for n0 in range(0, acc_ref.shape[1], ACC_COLS):
        cols = slice(n0, n0 + ACC_COLS)
        acc_ref[:, cols] += jnp.dot(u, wd_ref[:, cols], preferred_element_type=F32)

    @pl.when(f == pl.num_programs(1) - 1)
    def _():
        _row_tile_copy(x1_hbm, x1_buf, sem, tm).wait()
        _residual_layer_norm(acc_ref, x1_buf, gate_ref, lng_ref, lnb_ref, alpha,
                             on_slab=lambda i: _slab_writeback(acc_ref, out_hbm, out_sems, tm, i).start())
        for i in range(tm // LN_ROWS):
            _slab_writeback(acc_ref, out_hbm, out_sems, tm, i).wait()


def _ffn(h2, x1, gate, w_up, w_down, ln_g, ln_b, seq, alpha):
    m, d = x1.shape
    ff = w_up.shape[1]
    tm, tf = 512, 1024
    tiles_per_batch = seq // tm
    mod_map = lambda i, f: (i // tiles_per_batch, 0, 0)
    const = lambda i, f: (0, 0)
    row_tile = lambda i, f: (i, 0)
    return pl.pallas_call(
        functools.partial(_ffn_body, alpha=alpha, tm=tm),
        grid=(m // tm, ff // tf),
        in_specs=[pl.BlockSpec((tm, d), row_tile),
                  pl.BlockSpec(memory_space=pl.ANY),
                  pl.BlockSpec((1, 1, d), mod_map),
                  pl.BlockSpec((d, tf), lambda i, f: (0, f)),
                  pl.BlockSpec((tf, d), lambda i, f: (f, 0)),
                  pl.BlockSpec((1, d), const),
                  pl.BlockSpec((1, d), const)],
        out_specs=pl.BlockSpec(memory_space=pl.ANY),
        out_shape=jax.ShapeDtypeStruct((m, d), F32),
        scratch_shapes=[pltpu.VMEM((tm, d), F32), pltpu.VMEM((tm, d), F32), pltpu.SemaphoreType.DMA(()),
                        pltpu.SemaphoreType.DMA((tm // LN_ROWS,))],
        compiler_params=_cparams(("arbitrary", "arbitrary")),
        name="ffn",
    )(h2, x1, gate, w_up, w_down, ln_g, ln_b)


def _head_expansion(lane_offset):
    rows = jnp.arange(DT_PAD)[:, None]
    heads = jnp.arange(D_SSM)[None, :] // HEAD_DIM
    e = (rows == heads + lane_offset).astype(BF16)
    return jnp.tile(e, (3, 1))


def _pad_rows(a, rows):
    return jnp.pad(a, ((0, rows - a.shape[0]), (0, 0)))


def kernel(x, c, w_ada, b_ada, w_in, ssm_conv_w, ssm_conv_b, ssm_dt_bias_f, ssm_dt_bias_b, ssm_a_log_f, ssm_a_log_b, ssm_d, ssm_norm_w, sc_conv_w, sc_norm_w, w_out, ln1_g, ln1_b, w_up, w_down, ln2_g, ln2_b):
    batch, seq, d = x.shape
    depth = w_ada.shape[0]
    alpha = float((2 * depth) ** 0.25)
    m = batch * seq
    dt_lo = D_SSM + D_XBC
    dt_hi = dt_lo + 2 * N_HEADS
    e3f = _head_expansion(0)
    e3b = _head_expansion(N_HEADS)
    c_pad = _pad_rows(c, HALO)
    x2 = x.reshape(m, d)

    for l in range(depth):
        mod = _ada(c_pad, w_ada[l], b_ada[l][None, :])[:batch]
        shift1, scale1, gate1, shift2, scale2, gate2 = [
            t[:, None, :] for t in jnp.split(mod, N_MOD, axis=-1)]

        w_t = w_in[l].T.astype(BF16)
        w_dt_t = jnp.pad(w_t[dt_lo:dt_hi], ((0, DT_PAD - 2 * N_HEADS), (0, 0)))
        p_xbc, p_rest, dt = _inproj(x2, scale1, shift1, w_t, w_dt_t, seq)

        zeros_q = jnp.zeros((DT_PAD - 2 * N_HEADS,), F32)
        bias_q = jnp.concatenate([ssm_dt_bias_f[l], ssm_dt_bias_b[l], zeros_q])[None, :]
        alog_q = jnp.concatenate([ssm_a_log_f[l], ssm_a_log_b[l], zeros_q])[None, :]
        act, yoffb, dtq = _ssd_bwd(p_xbc, dt, _pad_rows(ssm_conv_w[l], HALO), ssm_conv_b[l][None, :],
                                   bias_q, alog_q, e3b, batch, seq)
        dskip_x = jnp.repeat(ssm_d[l], HEAD_DIM)[None, :]
        y_ssm = _ssd_fwd(act, p_rest, dtq, yoffb, alog_q, dskip_x, ssm_norm_w[l][None, :], e3f, batch, seq)
        y_sc = _shortconv(p_rest, _pad_rows(sc_conv_w[l], HALO), sc_norm_w[l][None, :], seq)

        x1, h2 = _outproj(y_ssm, y_sc, w_out[l].astype(BF16), x2, gate1, scale2, shift2,
                          ln1_g[l][None, :], ln1_b[l][None, :], seq, alpha)
        x2 = _ffn(h2, x1, gate2, w_up[l].astype(BF16), w_down[l].astype(BF16),
                  ln2_g[l][None, :], ln2_b[l][None, :], seq, alpha)
    return x2.reshape(batch, seq, d)
```

```python
import functools

import jax
import jax.numpy as jnp
from jax import lax
from jax.experimental import pallas as pl
from jax.experimental.pallas import tpu as pltpu

F32 = jnp.float32
BF16 = jnp.bfloat16
HIGHEST = lax.Precision.HIGHEST

D_MODEL = 4096
D_SSM = 2048
D_SC = 2048
HEAD_DIM = 64
N_HEADS = 32
N_GROUPS = 8
HEADS_PER_GROUP = 4
D_STATE = 128
CHUNK = 128
SSM_CONV = 5
SC_CONV = 3
SC_GROUPS = 16
D_XBC = D_SSM + 2 * N_GROUPS * D_STATE
D_FF = 4 * D_MODEL
N_MOD = 6
LN_EPS = 1e-5
RMS_EPS = 1e-5
GROUP_W = HEADS_PER_GROUP * HEAD_DIM
D_MAIN = D_XBC + D_SSM + 3 * D_SC
DT_PAD = 128
HALO = 8
HALO_BF16 = 16
LANES = 128
LN_ROWS = 64
ACC_COLS = 512

COL_Z = 0
COL_H = 1
COL_B = 2
COL_C = 3

VMEM_LIMIT = 62 * 1024 * 1024


def _cparams(sem):
    return pltpu.CompilerParams(dimension_semantics=sem, vmem_limit_bytes=VMEM_LIMIT)


def _sigmoid(v):
    return 1.0 / (1.0 + jnp.exp(-v))


def _silu(v):
    return v * _sigmoid(v)


def _softplus(v):
    return jnp.maximum(v, 0.0) + jnp.log1p(jnp.exp(-jnp.abs(v)))


def _layer_norm(r, g, b):
    mu = jnp.mean(r, axis=-1, keepdims=True)
    d = r - mu
    var = jnp.mean(d * d, axis=-1, keepdims=True)
    return d * lax.rsqrt(var + LN_EPS) * g + b


def _ada_body(c_ref, w_ref, b_ref, o_ref):
    o_ref[...] = jnp.dot(_silu(c_ref[...]).astype(BF16), w_ref[...].astype(BF16),
                         preferred_element_type=F32) + b_ref[...]


def _ada(c_pad, w_ada, b_ada):
    rows, d = c_pad.shape
    n = w_ada.shape[1]
    tn = 512
    return pl.pallas_call(
        _ada_body,
        grid=(n // tn,),
        in_specs=[pl.BlockSpec((rows, d), lambda j: (0, 0)),
                  pl.BlockSpec((d, tn), lambda j: (0, j)),
                  pl.BlockSpec((1, tn), lambda j: (0, j))],
        out_specs=pl.BlockSpec((rows, tn), lambda j: (0, j)),
        out_shape=jax.ShapeDtypeStruct((rows, n), F32),
        compiler_params=_cparams(("arbitrary",)),
        name="ada",
    )(c_pad, w_ada, b_ada)


def _dot_nt(a, b_t):
    return lax.dot_general(a, b_t, (((1,), (1,)), ((), ())), preferred_element_type=F32)


def _inproj_body(x_ref, sc_ref, sh_ref, w_ref, wdt_ref, xbc_ref, rest_ref, dt_ref, h_scr, *, xbc_tiles):
    j = pl.program_id(1)

    @pl.when(j == 0)
    def _():
        h = (x_ref[...] * (1.0 + sc_ref[0]) + sh_ref[0]).astype(BF16)
        h_scr[...] = h
        dt_ref[...] = _dot_nt(h, wdt_ref[...])

    @pl.when(j < xbc_tiles)
    def _():
        xbc_ref[...] = _dot_nt(h_scr[...], w_ref[...])

    @pl.when(j >= xbc_tiles)
    def _():
        rest_ref[...] = _dot_nt(h_scr[...], w_ref[...]).astype(BF16)


def _inproj(x2, scale, shift, w_t, w_dt_t, seq):
    m, d = x2.shape
    n = D_MAIN
    tm, tn = 1024, 512
    tiles_per_batch = seq // tm
    xbc_tiles = D_XBC // tn
    z_tiles = D_SSM // tn
    hbc_start = D_SSM + D_XBC + 2 * N_HEADS

    def w_row_start(j):
        start = jnp.where(j < xbc_tiles, D_SSM + j * tn,
                          jnp.where(j < xbc_tiles + z_tiles, (j - xbc_tiles) * tn,
                                    hbc_start + (j - xbc_tiles - z_tiles) * tn))
        return pl.multiple_of(start, 2 * N_HEADS)
    return pl.pallas_call(
        functools.partial(_inproj_body, xbc_tiles=xbc_tiles),
        grid=(m // tm, n // tn),
        in_specs=[pl.BlockSpec((tm, d), lambda i, j: (i, 0)),
                  pl.BlockSpec((1, 1, d), lambda i, j: (i // tiles_per_batch, 0, 0)),
                  pl.BlockSpec((1, 1, d), lambda i, j: (i // tiles_per_batch, 0, 0)),
                  pl.BlockSpec((pl.Element(tn), pl.Element(d)), lambda i, j: (w_row_start(j), 0)),
                  pl.BlockSpec((DT_PAD, d), lambda i, j: (0, 0))],
        out_specs=[pl.BlockSpec((tm, tn), lambda i, j: (i, jnp.minimum(j, xbc_tiles - 1))),
                   pl.BlockSpec((tm, tn), lambda i, j: (i, jnp.maximum(j - xbc_tiles, 0))),
                   pl.BlockSpec((tm, DT_PAD), lambda i, j: (i, 0))],
        out_shape=[jax.ShapeDtypeStruct((m, D_XBC), F32),
                   jax.ShapeDtypeStruct((m, n - D_XBC), BF16),
                   jax.ShapeDtypeStruct((m, DT_PAD), F32)],
        scratch_shapes=[pltpu.VMEM((tm, d), BF16)],
        compiler_params=_cparams(("arbitrary", "arbitrary")),
        name="inproj",
    )(x2, scale, shift, w_t, w_dt_t)


def _expand_heads(q, e3_ref):
    hi = q.astype(BF16)
    r1 = q - hi.astype(F32)
    mid = r1.astype(BF16)
    lo = (r1 - mid.astype(F32)).astype(BF16)
    pieces = jnp.concatenate([hi, mid, lo], axis=1)
    return jnp.dot(pieces, e3_ref[...], preferred_element_type=F32)


def _chunk_decay_terms(dtq, alog_ref):
    row = lax.broadcasted_iota(jnp.int32, (CHUNK, CHUNK), 0)
    col = lax.broadcasted_iota(jnp.int32, (CHUNK, CHUNK), 1)
    tril = col <= row
    triu = col >= row
    da = dtq * (-jnp.exp(alog_ref[...]))
    pre = jnp.dot(tril.astype(F32), da, preferred_element_type=F32, precision=HIGHEST)
    suf = jnp.dot(triu.astype(F32), da, preferred_element_type=F32, precision=HIGHEST)
    acum = jnp.where(col < N_HEADS, pre, suf)
    return acum, tril, triu


def _ssd_bwd_body(cur_ref, prev_ref, next_ref, dt_ref, cw_ref, cb_ref, bias_ref, alog_ref, e3_ref,
                  act_ref, yoff_ref, dtq_ref, st_ref, *, n_chunks):
    s = pl.program_id(1)
    c = n_chunks - 1 - s

    @pl.when(s == 0)
    def _():
        st_ref[...] = jnp.zeros_like(st_ref)

    lane_tile = 512
    rows = CHUNK + 2 * HALO
    has_prev = c > 0
    has_next = c < n_chunks - 1
    for t in range(D_XBC // lane_tile):
        sl = slice(t * lane_tile, (t + 1) * lane_tile)
        stack = jnp.concatenate([jnp.where(has_prev, prev_ref[:, sl], 0.0), cur_ref[:, sl],
                                 jnp.where(has_next, next_ref[:, sl], 0.0)], axis=0)
        w = cw_ref[:, sl]
        conv = cb_ref[:, sl] + w[SSM_CONV // 2:SSM_CONV // 2 + 1, :] * stack[HALO:HALO + CHUNK, :]
        for k in range(SSM_CONV):
            shift = SSM_CONV // 2 - k
            if shift != 0:
                tap = pltpu.roll(stack, shift % rows, 0)[HALO:HALO + CHUNK, :]
                conv = conv + w[k:k + 1, :] * tap
        act_ref[:, sl] = _silu(conv)

    dtq = _softplus(dt_ref[...] + bias_ref[...])
    dtq_ref[...] = dtq
    acum, _, _ = _chunk_decay_terms(dtq, alog_ref)
    lane = lax.broadcasted_iota(jnp.int32, (CHUNK, DT_PAD), 1)
    acum = jnp.where((lane >= N_HEADS) & (lane < 2 * N_HEADS), acum, 0.0)
    total = acum[0:1, :]
    eb_x = _expand_heads(jnp.exp(acum), e3_ref)
    wb_x = _expand_heads(dtq * jnp.exp(total - acum), e3_ref)
    dec_x = eb_x[0:1, :]

    for g in range(N_GROUPS):
        ch = slice(g * GROUP_W, (g + 1) * GROUP_W)
        b_g = act_ref[:, D_SSM + g * D_STATE:D_SSM + (g + 1) * D_STATE].astype(BF16)
        c_g = act_ref[:, D_SSM + N_GROUPS * D_STATE + g * D_STATE:
                      D_SSM + N_GROUPS * D_STATE + (g + 1) * D_STATE].astype(BF16)
        st = st_ref[g]
        yoff_ref[:, ch] = jnp.dot(c_g, st.astype(BF16), preferred_element_type=F32) * eb_x[:, ch]
        xw = (act_ref[:, ch] * wb_x[:, ch]).astype(BF16)
        upd = lax.dot_general(b_g, xw, (((0,), (0,)), ((), ())), preferred_element_type=F32)
        st_ref[g] = st * dec_x[:, ch] + upd


def _ssd_bwd(p_main, dt, conv_w, conv_b, bias_q, alog_q, e3b, batch, seq):
    m = p_main.shape[0]
    nc = seq // CHUNK
    rb = CHUNK // HALO
    last_halo = m // HALO - 1

    def cur_map(b, s):
        return (b * nc + (nc - 1 - s), 0)

    def prev_map(b, s):
        return (jnp.maximum((b * nc + (nc - 1 - s)) * rb - 1, 0), 0)

    def next_map(b, s):
        return (jnp.minimum((b * nc + (nc - 1 - s) + 1) * rb, last_halo), 0)

    const = lambda b, s: (0, 0)
    return pl.pallas_call(
        functools.partial(_ssd_bwd_body, n_chunks=nc),
        grid=(batch, nc),
        in_specs=[pl.BlockSpec((CHUNK, D_XBC), cur_map),
                  pl.BlockSpec((HALO, D_XBC), prev_map),
                  pl.BlockSpec((HALO, D_XBC), next_map),
                  pl.BlockSpec((CHUNK, DT_PAD), cur_map),
                  pl.BlockSpec((HALO, D_XBC), const),
                  pl.BlockSpec((1, D_XBC), const),
                  pl.BlockSpec((1, DT_PAD), const),
                  pl.BlockSpec((1, DT_PAD), const),
                  pl.BlockSpec((3 * DT_PAD, D_SSM), const)],
        out_specs=[pl.BlockSpec((CHUNK, D_XBC), cur_map),
                   pl.BlockSpec((CHUNK, D_SSM), cur_map),
                   pl.BlockSpec((CHUNK, DT_PAD), cur_map)],
        out_shape=[jax.ShapeDtypeStruct((m, D_XBC), F32),
                   jax.ShapeDtypeStruct((m, D_SSM), F32),
                   jax.ShapeDtypeStruct((m, DT_PAD), F32)],
        scratch_shapes=[pltpu.VMEM((N_GROUPS, D_STATE, GROUP_W), F32)],
        compiler_params=_cparams(("arbitrary", "arbitrary")),
        name="ssd_bwd",
    )(p_main, p_main, p_main, dt, conv_w, conv_b, bias_q, alog_q, e3b)


def _ssd_fwd_body(act_ref, z_ref, dtq_ref, yoffb_ref, alog_ref, dskip_ref, normw_ref, e3_ref,
                  y_ref, st_ref):
    @pl.when(pl.program_id(1) == 0)
    def _():
        st_ref[...] = jnp.zeros_like(st_ref)

    dtq = dtq_ref[...]
    acum, tril, triu = _chunk_decay_terms(dtq, alog_ref)
    acum_t = acum.T
    dtq_t = dtq.T
    lane = lax.broadcasted_iota(jnp.int32, (CHUNK, DT_PAD), 1)
    acum_f = jnp.where(lane < N_HEADS, acum, 0.0)
    total = acum_f[CHUNK - 1:CHUNK, :]
    ef_x = _expand_heads(jnp.exp(acum_f), e3_ref)
    wf_x = _expand_heads(dtq * jnp.exp(total - acum_f), e3_ref)
    dec_x = ef_x[CHUNK - 1:CHUNK, :]
    head_of_lane = lax.broadcasted_iota(jnp.int32, (CHUNK, GROUP_W), 1) // HEAD_DIM
    neg_inf = jnp.float32(-jnp.inf)

    for g in range(N_GROUPS):
        ch = slice(g * GROUP_W, (g + 1) * GROUP_W)
        b_g = act_ref[:, D_SSM + g * D_STATE:D_SSM + (g + 1) * D_STATE].astype(BF16)
        c_g = act_ref[:, D_SSM + N_GROUPS * D_STATE + g * D_STATE:
                      D_SSM + N_GROUPS * D_STATE + (g + 1) * D_STATE].astype(BF16)
        x_g = act_ref[:, ch]
        scores = lax.dot_general(c_g, b_g, (((1,), (1,)), ((), ())), preferred_element_type=F32)

        mats = []
        blocks = []
        for r in range(HEADS_PER_GROUP):
            hf = g * HEADS_PER_GROUP + r
            hb = N_HEADS + hf
            seg_f = acum[:, hf:hf + 1] - acum_t[hf:hf + 1, :]
            seg_b = acum[:, hb:hb + 1] - acum_t[hb:hb + 1, :]
            lf = jnp.exp(jnp.where(tril, seg_f, neg_inf)) * dtq_t[hf:hf + 1, :]
            ub = jnp.exp(jnp.where(triu, seg_b, neg_inf)) * dtq_t[hb:hb + 1, :]
            mats.append((scores * (lf + ub)).astype(BF16))
            blocks.append(jnp.where(head_of_lane == r, x_g, 0.0).astype(BF16))
        y_diag = jnp.dot(jnp.concatenate(mats, axis=1), jnp.concatenate(blocks, axis=0),
                         preferred_element_type=F32)

        st = st_ref[g]
        y_off = jnp.dot(c_g, st.astype(BF16), preferred_element_type=F32) * ef_x[:, ch]
        y = y_diag + y_off + yoffb_ref[:, ch] + dskip_ref[:, ch] * x_g

        xw = (x_g * wf_x[:, ch]).astype(BF16)
        upd = lax.dot_general(b_g, xw, (((0,), (0,)), ((), ())), preferred_element_type=F32)
        st_ref[g] = st * dec_x[:, ch] + upd

        y = y * _silu(z_ref[:, ch].astype(F32))
        ms = jnp.mean(y * y, axis=-1, keepdims=True)
        y_ref[:, ch] = (y * lax.rsqrt(ms + RMS_EPS) * normw_ref[:, ch]).astype(BF16)


def _ssd_fwd(act, p_rest, dtq, yoffb, alog_q, dskip_x, norm_w, e3f, batch, seq):
    m = act.shape[0]
    nc = seq // CHUNK
    cur = lambda b, s: (b * nc + s, 0)
    const = lambda b, s: (0, 0)
    return pl.pallas_call(
        _ssd_fwd_body,
        grid=(batch, nc),
        in_specs=[pl.BlockSpec((CHUNK, D_XBC), cur),
                  pl.BlockSpec((CHUNK, D_SSM), lambda b, s: (b * nc + s, COL_Z)),
                  pl.BlockSpec((CHUNK, DT_PAD), cur),
                  pl.BlockSpec((CHUNK, D_SSM), cur),
                  pl.BlockSpec((1, DT_PAD), const),
                  pl.BlockSpec((1, D_SSM), const),
                  pl.BlockSpec((1, D_SSM), const),
                  pl.BlockSpec((3 * DT_PAD, D_SSM), const)],
        out_specs=pl.BlockSpec((CHUNK, D_SSM), cur),
        out_shape=jax.ShapeDtypeStruct((m, D_SSM), BF16),
        scratch_shapes=[pltpu.VMEM((N_GROUPS, D_STATE, GROUP_W), F32)],
        compiler_params=_cparams(("arbitrary", "arbitrary")),
        name="ssd_fwd",
    )(act, p_rest, dtq, yoffb, alog_q, dskip_x, norm_w, e3f)


def _shortconv_body(h_ref, b_ref, c_ref, hp_ref, cp_ref, hn_ref, cn_ref, cw_ref, nw_ref, y_ref,
                    *, tiles_per_seq, rows):
    i = pl.program_id(0)
    t = i % tiles_per_seq
    has_prev = t > 0
    has_next = t < tiles_per_seq - 1
    lane_tile = 512
    row = lax.broadcasted_iota(jnp.int32, (rows, lane_tile), 0)
    group = D_SC // SC_GROUPS
    for q in range(D_SC // lane_tile):
        sl = slice(q * lane_tile, (q + 1) * lane_tile)
        v = c_ref[:, sl].astype(F32) * h_ref[:, sl].astype(F32)
        last = HALO_BF16 - 1
        vp = jnp.where(has_prev, cp_ref[:, sl].astype(F32)[last:, :] * hp_ref[:, sl].astype(F32)[last:, :], 0.0)
        vn = jnp.where(has_next, cn_ref[:, sl].astype(F32)[0:1, :] * hn_ref[:, sl].astype(F32)[0:1, :], 0.0)
        down = jnp.where(row == 0, vp, pltpu.roll(v, 1, 0))
        up = jnp.where(row == rows - 1, vn, pltpu.roll(v, rows - 1, 0))
        w = cw_ref[:, sl]
        y = b_ref[:, sl].astype(F32) * (w[0:1, :] * down + w[1:2, :] * v + w[2:3, :] * up)
        for k in range(lane_tile // group):
            gs = slice(k * group, (k + 1) * group)
            yg = y[:, gs]
            ms = jnp.mean(yg * yg, axis=-1, keepdims=True)
            lo = q * lane_tile + k * group
            y_ref[:, lo:lo + group] = (yg * lax.rsqrt(ms + RMS_EPS) * nw_ref[:, lo:lo + group]).astype(BF16)


def _shortconv(p_rest, conv_w, norm_w, seq):
    m = p_rest.shape[0]
    rows = 256
    tiles_per_seq = seq // rows
    rb = rows // HALO_BF16
    last_halo = m // HALO_BF16 - 1

    def cur(col):
        return lambda i: (i, col)

    def prev(col):
        return lambda i: (jnp.maximum(i * rb - 1, 0), col)

    def nxt(col):
        return lambda i: (jnp.minimum((i + 1) * rb, last_halo), col)

    const = lambda i: (0, 0)
    return pl.pallas_call(
        functools.partial(_shortconv_body, tiles_per_seq=tiles_per_seq, rows=rows),
        grid=(m // rows,),
        in_specs=[pl.BlockSpec((rows, D_SC), cur(COL_H)),
                  pl.BlockSpec((rows, D_SC), cur(COL_B)),
                  pl.BlockSpec((rows, D_SC), cur(COL_C)),
                  pl.BlockSpec((HALO_BF16, D_SC), prev(COL_H)),
                  pl.BlockSpec((HALO_BF16, D_SC), prev(COL_C)),
                  pl.BlockSpec((HALO_BF16, D_SC), nxt(COL_H)),
                  pl.BlockSpec((HALO_BF16, D_SC), nxt(COL_C)),
                  pl.BlockSpec((HALO, D_SC), const),
                  pl.BlockSpec((1, D_SC), const)],
        out_specs=pl.BlockSpec((rows, D_SC), lambda i: (i, 0)),
        out_shape=jax.ShapeDtypeStruct((m, D_SC), BF16),
        compiler_params=_cparams(("arbitrary",)),
        name="shortconv",
    )(p_rest, p_rest, p_rest, p_rest, p_rest, p_rest, p_rest, conv_w, norm_w)


def _residual_layer_norm(o_ref, x_ref, gate_ref, lng_ref, lnb_ref, alpha, modulated=None, on_slab=None):
    width = o_ref.shape[1]
    lane_tiles = [slice(t, t + LANES) for t in range(0, width, LANES)]

    def slab(i, carry):
        sl = pl.ds(pl.multiple_of(i * LN_ROWS, LN_ROWS), LN_ROWS)
        acc = jnp.zeros((LN_ROWS, LANES), F32)
        for lt in lane_tiles:
            r = alpha * x_ref[sl, lt] + (1.0 + gate_ref[0, :, lt]) * o_ref[sl, lt]
            o_ref[sl, lt] = r
            acc = acc + r
        mean = jnp.broadcast_to(jnp.sum(acc, axis=-1, keepdims=True) * (1.0 / width), (LN_ROWS, LANES))
        acc = jnp.zeros((LN_ROWS, LANES), F32)
        for lt in lane_tiles:
            d = o_ref[sl, lt] - mean
            acc = acc + d * d
        var = jnp.sum(acc, axis=-1, keepdims=True) * (1.0 / width)
        rstd = jnp.broadcast_to(lax.rsqrt(var + LN_EPS), (LN_ROWS, LANES))
        for lt in lane_tiles:
            y = (o_ref[sl, lt] - mean) * rstd * lng_ref[:, lt] + lnb_ref[:, lt]
            o_ref[sl, lt] = y
            if modulated is not None:
                h_ref, sc_ref, sh_ref = modulated
                h_ref[sl, lt] = (y * (1.0 + sc_ref[0, :, lt]) + sh_ref[0, :, lt]).astype(BF16)
        if on_slab is not None:
            on_slab(i)
        return carry

    lax.fori_loop(0, o_ref.shape[0] // LN_ROWS, slab, 0)


def _row_tile_copy(hbm_ref, buf_ref, sem, tm):
    rows = pl.ds(pl.multiple_of(pl.program_id(0) * tm, tm), tm)
    return pltpu.make_async_copy(hbm_ref.at[rows, :], buf_ref, sem)


def _outproj_body(ys_ref, yc_ref, w_ref, x_hbm, gate_ref, sc_ref, sh_ref, lng_ref, lnb_ref, x1_ref, h2_ref,
                  x_buf, sem, *, alpha, tm, tn):
    n = pl.program_id(1)

    @pl.when(n == 0)
    def _():
        _row_tile_copy(x_hbm, x_buf, sem, tm).start()

    k_ssm = ys_ref.shape[1]
    cols = pl.ds(pl.multiple_of(n * tn, tn), tn)
    x1_ref[:, cols] = _dot_nt(ys_ref[...], w_ref[:, :k_ssm]) + _dot_nt(yc_ref[...], w_ref[:, k_ssm:])

    @pl.when(n == pl.num_programs(1) - 1)
    def _():
        _row_tile_copy(x_hbm, x_buf, sem, tm).wait()
        _residual_layer_norm(x1_ref, x_buf, gate_ref, lng_ref, lnb_ref, alpha,
                             modulated=(h2_ref, sc_ref, sh_ref))


def _outproj(y_ssm, y_sc, w_out_t, x2, gate, scale2, shift2, ln_g, ln_b, seq, alpha):
    m, d = x2.shape
    tm, tn = 512, 1024
    tiles_per_batch = seq // tm
    mod_map = lambda i, n: (i // tiles_per_batch, 0, 0)
    const = lambda i, n: (0, 0)
    row_tile = lambda i, n: (i, 0)
    return pl.pallas_call(
        functools.partial(_outproj_body, alpha=alpha, tm=tm, tn=tn),
        grid=(m // tm, d // tn),
        in_specs=[pl.BlockSpec((tm, D_SSM), row_tile),
                  pl.BlockSpec((tm, D_SC), row_tile),
                  pl.BlockSpec((tn, D_SSM + D_SC), lambda i, n: (n, 0)),
                  pl.BlockSpec(memory_space=pl.ANY),
                  pl.BlockSpec((1, 1, d), mod_map),
                  pl.BlockSpec((1, 1, d), mod_map),
                  pl.BlockSpec((1, 1, d), mod_map),
                  pl.BlockSpec((1, d), const),
                  pl.BlockSpec((1, d), const)],
        out_specs=[pl.BlockSpec((tm, d), row_tile),
                   pl.BlockSpec((tm, d), row_tile)],
        out_shape=[jax.ShapeDtypeStruct((m, d), F32),
                   jax.ShapeDtypeStruct((m, d), BF16)],
        scratch_shapes=[pltpu.VMEM((tm, d), F32), pltpu.SemaphoreType.DMA(())],
        compiler_params=_cparams(("arbitrary", "arbitrary")),
        name="outproj",
    )(y_ssm, y_sc, w_out_t, x2, gate, scale2, shift2, ln_g, ln_b)


def _slab_writeback(acc_ref, out_hbm, sems, tm, i):
    src = pl.ds(pl.multiple_of(i * LN_ROWS, LN_ROWS), LN_ROWS)
    dst = pl.ds(pl.multiple_of(pl.program_id(0) * tm + i * LN_ROWS, LN_ROWS), LN_ROWS)
    return pltpu.make_async_copy(acc_ref.at[src, :], out_hbm.at[dst, :], sems.at[i])


def _ffn_body(h2_ref, x1_hbm, gate_ref, wu_ref, wd_ref, lng_ref, lnb_ref, out_hbm, acc_ref, x1_buf, sem,
              out_sems, *, alpha, tm):
    f = pl.program_id(1)

    @pl.when(f == 0)
    def _():
        _row_tile_copy(x1_hbm, x1_buf, sem, tm).start()
        acc_ref[...] = jnp.zeros_like(acc_ref)

    u = jnp.maximum(jnp.dot(h2_ref[...], wu_ref[...], preferred_element_type=F32), 0.0)
    u = (u * u).astype(BF16)
    for n0 in range(0, acc_ref.shape[1], ACC_COLS):
        cols = slice(n0, n0 + ACC_COLS)
        acc_ref[:, cols] += jnp.dot(u, wd_ref[:, cols], preferred_element_type=F32)

    @pl.when(f == pl.num_programs(1) - 1)
    def _():
        _row_tile_copy(x1_hbm, x1_buf, sem, tm).wait()
        _residual_layer_norm(acc_ref, x1_buf, gate_ref, lng_ref, lnb_ref, alpha,
                             on_slab=lambda i: _slab_writeback(acc_ref, out_hbm, out_sems, tm, i).start())
        for i in range(tm // LN_ROWS):
            _slab_writeback(acc_ref, out_hbm, out_sems, tm, i).wait()


def _ffn(h2, x1, gate, w_up, w_down, ln_g, ln_b, seq, alpha):
    m, d = x1.shape
    ff = w_up.shape[1]
    tm, tf = 512, 1024
    tiles_per_batch = seq // tm
    mod_map = lambda i, f: (i // tiles_per_batch, 0, 0)
    const = lambda i, f: (0, 0)
    row_tile = lambda i, f: (i, 0)
    return pl.pallas_call(
        functools.partial(_ffn_body, alpha=alpha, tm=tm),
        grid=(m // tm, ff // tf),
        in_specs=[pl.BlockSpec((tm, d), row_tile),
                  pl.BlockSpec(memory_space=pl.ANY),
                  pl.BlockSpec((1, 1, d), mod_map),
                  pl.BlockSpec((d, tf), lambda i, f: (0, f)),
                  pl.BlockSpec((tf, d), lambda i, f: (f, 0)),
                  pl.BlockSpec((1, d), const),
                  pl.BlockSpec((1, d), const)],
        out_specs=pl.BlockSpec(memory_space=pl.ANY),
        out_shape=jax.ShapeDtypeStruct((m, d), F32),
        scratch_shapes=[pltpu.VMEM((tm, d), F32), pltpu.VMEM((tm, d), F32), pltpu.SemaphoreType.DMA(()),
                        pltpu.SemaphoreType.DMA((tm // LN_ROWS,))],
        compiler_params=_cparams(("arbitrary", "arbitrary")),
        name="ffn",
    )(h2, x1, gate, w_up, w_down, ln_g, ln_b)


def _head_expansion(lane_offset):
    rows = jnp.arange(DT_PAD)[:, None]
    heads = jnp.arange(D_SSM)[None, :] // HEAD_DIM
    e = (rows == heads + lane_offset).astype(BF16)
    return jnp.tile(e, (3, 1))


def _pad_rows(a, rows):
    return jnp.pad(a, ((0, rows - a.shape[0]), (0, 0)))


def kernel(x, c, w_ada, b_ada, w_in, ssm_conv_w, ssm_conv_b, ssm_dt_bias_f, ssm_dt_bias_b, ssm_a_log_f, ssm_a_log_b, ssm_d, ssm_norm_w, sc_conv_w, sc_norm_w, w_out, ln1_g, ln1_b, w_up, w_down, ln2_g, ln2_b):
    batch, seq, d = x.shape
    depth = w_ada.shape[0]
    alpha = float((2 * depth) ** 0.25)
    m = batch * seq
    dt_lo = D_SSM + D_XBC
    dt_hi = dt_lo + 2 * N_HEADS
    e3f = _head_expansion(0)
    e3b = _head_expansion(N_HEADS)
    c_pad = _pad_rows(c, HALO)
    x2 = x.reshape(m, d)

    for l in range(depth):
        mod = _ada(c_pad, w_ada[l], b_ada[l][None, :])[:batch]
        shift1, scale1, gate1, shift2, scale2, gate2 = [
            t[:, None, :] for t in jnp.split(mod, N_MOD, axis=-1)]

        w_t = w_in[l].T.astype(BF16)
        w_dt_t = jnp.pad(w_t[dt_lo:dt_hi], ((0, DT_PAD - 2 * N_HEADS), (0, 0)))
        p_xbc, p_rest, dt = _inproj(x2, scale1, shift1, w_t, w_dt_t, seq)

        zeros_q = jnp.zeros((DT_PAD - 2 * N_HEADS,), F32)
        bias_q = jnp.concatenate([ssm_dt_bias_f[l], ssm_dt_bias_b[l], zeros_q])[None, :]
        alog_q = jnp.concatenate([ssm_a_log_f[l], ssm_a_log_b[l], zeros_q])[None, :]
        act, yoffb, dtq = _ssd_bwd(p_xbc, dt, _pad_rows(ssm_conv_w[l], HALO), ssm_conv_b[l][None, :],
                                   bias_q, alog_q, e3b, batch, seq)
        dskip_x = jnp.repeat(ssm_d[l], HEAD_DIM)[None, :]
        y_ssm = _ssd_fwd(act, p_rest, dtq, yoffb, alog_q, dskip_x, ssm_norm_w[l][None, :], e3f, batch, seq)
        y_sc = _shortconv(p_rest, _pad_rows(sc_conv_w[l], HALO), sc_norm_w[l][None, :], seq)

        x1, h2 = _outproj(y_ssm, y_sc, w_out[l].T.astype(BF16), x2, gate1, scale2, shift2,
                          ln1_g[l][None, :], ln1_b[l][None, :], seq, alpha)
        x2 = _ffn(h2, x1, gate2, w_up[l].astype(BF16), w_down[l].astype(BF16),
                  ln2_g[l][None, :], ln2_b[l][None, :], seq, alpha)
    return x2.reshape(batch, seq, d)
```

```python
import functools

import jax
import jax.numpy as jnp
from jax import lax
from jax.experimental import pallas as pl
from jax.experimental.pallas import tpu as pltpu

F32 = jnp.float32
BF16 = jnp.bfloat16
HIGHEST = lax.Precision.HIGHEST

D_MODEL = 4096
D_SSM = 2048
D_SC = 2048
HEAD_DIM = 64
N_HEADS = 32
N_GROUPS = 8
HEADS_PER_GROUP = 4
D_STATE = 128
CHUNK = 128
SSM_CONV = 5
SC_CONV = 3
SC_GROUPS = 16
D_XBC = D_SSM + 2 * N_GROUPS * D_STATE
D_FF = 4 * D_MODEL
N_MOD = 6
LN_EPS = 1e-5
RMS_EPS = 1e-5
GROUP_W = HEADS_PER_GROUP * HEAD_DIM
D_MAIN = D_XBC + D_SSM + 3 * D_SC
DT_PAD = 128
HALO = 8
HALO_BF16 = 16
LANES = 128
LN_ROWS = 64
ACC_COLS = 512
X_SKEW = LANES
X_PAD = 4 * LANES

COL_Z = 0
COL_H = 1
COL_B = 2
COL_C = 3

VMEM_LIMIT = 63 * 1024 * 1024


def _cparams(sem):
    return pltpu.CompilerParams(dimension_semantics=sem, vmem_limit_bytes=VMEM_LIMIT)


def _sigmoid(v):
    return 1.0 / (1.0 + jnp.exp(-v))


def _silu(v):
    return v * _sigmoid(v)


def _softplus(v):
    return jnp.maximum(v, 0.0) + jnp.log1p(jnp.exp(-jnp.abs(v)))


def _layer_norm(r, g, b):
    mu = jnp.mean(r, axis=-1, keepdims=True)
    d = r - mu
    var = jnp.mean(d * d, axis=-1, keepdims=True)
    return d * lax.rsqrt(var + LN_EPS) * g + b


def _ada_body(c_ref, w_ref, b_ref, o_ref):
    o_ref[...] = jnp.dot(_silu(c_ref[...]).astype(BF16), w_ref[...].astype(BF16),
                         preferred_element_type=F32) + b_ref[...]


def _ada(c_pad, w_ada, b_ada):
    rows, d = c_pad.shape
    n = w_ada.shape[1]
    tn = 512
    return pl.pallas_call(
        _ada_body,
        grid=(n // tn,),
        in_specs=[pl.BlockSpec((rows, d), lambda j: (0, 0)),
                  pl.BlockSpec((d, tn), lambda j: (0, j)),
                  pl.BlockSpec((1, tn), lambda j: (0, j))],
        out_specs=pl.BlockSpec((rows, tn), lambda j: (0, j)),
        out_shape=jax.ShapeDtypeStruct((rows, n), F32),
        compiler_params=_cparams(("arbitrary",)),
        name="ada",
    )(c_pad, w_ada, b_ada)


def _dot_nt(a, b_t):
    return lax.dot_general(a, b_t, (((1,), (1,)), ((), ())), preferred_element_type=F32)


def _inproj_body(x_ref, sc_ref, sh_ref, w_ref, wdt_ref, xbc_ref, rest_ref, dt_ref, h_scr, *, xbc_tiles):
    j = pl.program_id(1)

    @pl.when(j == 0)
    def _():
        h = (x_ref[...] * (1.0 + sc_ref[0]) + sh_ref[0]).astype(BF16)
        h_scr[...] = h
        dt_ref[...] = _dot_nt(h, wdt_ref[...])

    @pl.when(j < xbc_tiles)
    def _():
        xbc_ref[...] = _dot_nt(h_scr[...], w_ref[...])

    @pl.when(j >= xbc_tiles)
    def _():
        rest_ref[...] = _dot_nt(h_scr[...], w_ref[...]).astype(BF16)


def _inproj(x2, scale, shift, w_t, w_dt_t, seq):
    m, d = x2.shape
    n = D_MAIN
    tm, tn = 1024, 512
    tiles_per_batch = seq // tm
    xbc_tiles = D_XBC // tn
    z_tiles = D_SSM // tn
    hbc_start = D_SSM + D_XBC + 2 * N_HEADS

    def w_row_start(j):
        start = jnp.where(j < xbc_tiles, D_SSM + j * tn,
                          jnp.where(j < xbc_tiles + z_tiles, (j - xbc_tiles) * tn,
                                    hbc_start + (j - xbc_tiles - z_tiles) * tn))
        return pl.multiple_of(start, 2 * N_HEADS)
    return pl.pallas_call(
        functools.partial(_inproj_body, xbc_tiles=xbc_tiles),
        grid=(m // tm, n // tn),
        in_specs=[pl.BlockSpec((tm, d), lambda i, j: (i, 0)),
                  pl.BlockSpec((1, 1, d), lambda i, j: (i // tiles_per_batch, 0, 0)),
                  pl.BlockSpec((1, 1, d), lambda i, j: (i // tiles_per_batch, 0, 0)),
                  pl.BlockSpec((pl.Element(tn), pl.Element(d)), lambda i, j: (w_row_start(j), 0)),
                  pl.BlockSpec((DT_PAD, d), lambda i, j: (0, 0))],
        out_specs=[pl.BlockSpec((tm, tn), lambda i, j: (i, jnp.minimum(j, xbc_tiles - 1))),
                   pl.BlockSpec((tm, tn), lambda i, j: (i, jnp.maximum(j - xbc_tiles, 0))),
                   pl.BlockSpec((tm, DT_PAD), lambda i, j: (i, 0))],
        out_shape=[jax.ShapeDtypeStruct((m, D_XBC), F32),
                   jax.ShapeDtypeStruct((m, n - D_XBC), BF16),
                   jax.ShapeDtypeStruct((m, DT_PAD), F32)],
        scratch_shapes=[pltpu.VMEM((tm, d), BF16)],
        compiler_params=_cparams(("arbitrary", "arbitrary")),
        name="inproj",
    )(x2, scale, shift, w_t, w_dt_t)


def _expand_heads(q, e3_ref):
    hi = q.astype(BF16)
    r1 = q - hi.astype(F32)
    mid = r1.astype(BF16)
    lo = (r1 - mid.astype(F32)).astype(BF16)
    pieces = jnp.concatenate([hi, mid, lo], axis=1)
    return jnp.dot(pieces, e3_ref[...], preferred_element_type=F32)


def _chunk_decay_terms(dtq, alog_ref):
    row = lax.broadcasted_iota(jnp.int32, (CHUNK, CHUNK), 0)
    col = lax.broadcasted_iota(jnp.int32, (CHUNK, CHUNK), 1)
    tril = col <= row
    triu = col >= row
    da = dtq * (-jnp.exp(alog_ref[...]))
    pre = jnp.dot(tril.astype(F32), da, preferred_element_type=F32, precision=HIGHEST)
    suf = jnp.dot(triu.astype(F32), da, preferred_element_type=F32, precision=HIGHEST)
    acum = jnp.where(col < N_HEADS, pre, suf)
    return acum, tril, triu


def _ssd_bwd_body(cur_ref, prev_ref, next_ref, dt_ref, cw_ref, cb_ref, bias_ref, alog_ref, e3_ref,
                  act_ref, yoff_ref, dtq_ref, st_ref, *, n_chunks):
    s = pl.program_id(1)
    c = n_chunks - 1 - s

    @pl.when(s == 0)
    def _():
        st_ref[...] = jnp.zeros_like(st_ref)

    lane_tile = 512
    rows = CHUNK + 2 * HALO
    has_prev = c > 0
    has_next = c < n_chunks - 1
    for t in range(D_XBC // lane_tile):
        sl = slice(t * lane_tile, (t + 1) * lane_tile)
        stack = jnp.concatenate([jnp.where(has_prev, prev_ref[:, sl], 0.0), cur_ref[:, sl],
                                 jnp.where(has_next, next_ref[:, sl], 0.0)], axis=0)
        w = cw_ref[:, sl]
        conv = cb_ref[:, sl] + w[SSM_CONV // 2:SSM_CONV // 2 + 1, :] * stack[HALO:HALO + CHUNK, :]
        for k in range(SSM_CONV):
            shift = SSM_CONV // 2 - k
            if shift != 0:
                tap = pltpu.roll(stack, shift % rows, 0)[HALO:HALO + CHUNK, :]
                conv = conv + w[k:k + 1, :] * tap
        act_ref[:, sl] = _silu(conv)

    dtq = _softplus(dt_ref[...] + bias_ref[...])
    dtq_ref[...] = dtq
    acum, _, _ = _chunk_decay_terms(dtq, alog_ref)
    lane = lax.broadcasted_iota(jnp.int32, (CHUNK, DT_PAD), 1)
    acum = jnp.where((lane >= N_HEADS) & (lane < 2 * N_HEADS), acum, 0.0)
    total = acum[0:1, :]
    eb_x = _expand_heads(jnp.exp(acum), e3_ref)
    wb_x = _expand_heads(dtq * jnp.exp(total - acum), e3_ref)
    dec_x = eb_x[0:1, :]

    for g in range(N_GROUPS):
        ch = slice(g * GROUP_W, (g + 1) * GROUP_W)
        b_g = act_ref[:, D_SSM + g * D_STATE:D_SSM + (g + 1) * D_STATE].astype(BF16)
        c_g = act_ref[:, D_SSM + N_GROUPS * D_STATE + g * D_STATE:
                      D_SSM + N_GROUPS * D_STATE + (g + 1) * D_STATE].astype(BF16)
        st = st_ref[g]
        yoff_ref[:, ch] = jnp.dot(c_g, st.astype(BF16), preferred_element_type=F32) * eb_x[:, ch]
        xw = (act_ref[:, ch] * wb_x[:, ch]).astype(BF16)
        upd = lax.dot_general(b_g, xw, (((0,), (0,)), ((), ())), preferred_element_type=F32)
        st_ref[g] = st * dec_x[:, ch] + upd


def _ssd_bwd(p_main, dt, conv_w, conv_b, bias_q, alog_q, e3b, batch, seq):
    m = p_main.shape[0]
    nc = seq // CHUNK
    rb = CHUNK // HALO
    last_halo = m // HALO - 1

    def cur_map(b, s):
        return (b * nc + (nc - 1 - s), 0)

    def prev_map(b, s):
        return (jnp.maximum((b * nc + (nc - 1 - s)) * rb - 1, 0), 0)

    def next_map(b, s):
        return (jnp.minimum((b * nc + (nc - 1 - s) + 1) * rb, last_halo), 0)

    const = lambda b, s: (0, 0)
    return pl.pallas_call(
        functools.partial(_ssd_bwd_body, n_chunks=nc),
        grid=(batch, nc),
        in_specs=[pl.BlockSpec((CHUNK, D_XBC), cur_map),
                  pl.BlockSpec((HALO, D_XBC), prev_map),
                  pl.BlockSpec((HALO, D_XBC), next_map),
                  pl.BlockSpec((CHUNK, DT_PAD), cur_map),
                  pl.BlockSpec((HALO, D_XBC), const),
                  pl.BlockSpec((1, D_XBC), const),
                  pl.BlockSpec((1, DT_PAD), const),
                  pl.BlockSpec((1, DT_PAD), const),
                  pl.BlockSpec((3 * DT_PAD, D_SSM), const)],
        out_specs=[pl.BlockSpec((CHUNK, D_XBC), cur_map),
                   pl.BlockSpec((CHUNK, D_SSM), cur_map),
                   pl.BlockSpec((CHUNK, DT_PAD), cur_map)],
        out_shape=[jax.ShapeDtypeStruct((m, D_XBC), F32),
                   jax.ShapeDtypeStruct((m, D_SSM), F32),
                   jax.ShapeDtypeStruct((m, DT_PAD), F32)],
        scratch_shapes=[pltpu.VMEM((N_GROUPS, D_STATE, GROUP_W), F32)],
        compiler_params=_cparams(("arbitrary", "arbitrary")),
        name="ssd_bwd",
    )(p_main, p_main, p_main, dt, conv_w, conv_b, bias_q, alog_q, e3b)


def _ssd_fwd_body(act_ref, z_ref, dtq_ref, yoffb_ref, alog_ref, dskip_ref, normw_ref, e3_ref,
                  y_ref, st_ref):
    @pl.when(pl.program_id(1) == 0)
    def _():
        st_ref[...] = jnp.zeros_like(st_ref)

    dtq = dtq_ref[...]
    acum, tril, triu = _chunk_decay_terms(dtq, alog_ref)
    acum_t = acum.T
    dtq_t = dtq.T
    lane = lax.broadcasted_iota(jnp.int32, (CHUNK, DT_PAD), 1)
    acum_f = jnp.where(lane < N_HEADS, acum, 0.0)
    total = acum_f[CHUNK - 1:CHUNK, :]
    ef_x = _expand_heads(jnp.exp(acum_f), e3_ref)
    wf_x = _expand_heads(dtq * jnp.exp(total - acum_f), e3_ref)
    dec_x = ef_x[CHUNK - 1:CHUNK, :]
    head_of_lane = lax.broadcasted_iota(jnp.int32, (CHUNK, GROUP_W), 1) // HEAD_DIM
    neg_inf = jnp.float32(-jnp.inf)

    for g in range(N_GROUPS):
        ch = slice(g * GROUP_W, (g + 1) * GROUP_W)
        b_g = act_ref[:, D_SSM + g * D_STATE:D_SSM + (g + 1) * D_STATE].astype(BF16)
        c_g = act_ref[:, D_SSM + N_GROUPS * D_STATE + g * D_STATE:
                      D_SSM + N_GROUPS * D_STATE + (g + 1) * D_STATE].astype(BF16)
        x_g = act_ref[:, ch]
        scores = lax.dot_general(c_g, b_g, (((1,), (1,)), ((), ())), preferred_element_type=F32)

        mats = []
        blocks = []
        for r in range(HEADS_PER_GROUP):
            hf = g * HEADS_PER_GROUP + r
            hb = N_HEADS + hf
            seg_f = acum[:, hf:hf + 1] - acum_t[hf:hf + 1, :]
            seg_b = acum[:, hb:hb + 1] - acum_t[hb:hb + 1, :]
            lf = jnp.exp(jnp.where(tril, seg_f, neg_inf)) * dtq_t[hf:hf + 1, :]
            ub = jnp.exp(jnp.where(triu, seg_b, neg_inf)) * dtq_t[hb:hb + 1, :]
            mats.append((scores * (lf + ub)).astype(BF16))
            blocks.append(jnp.where(head_of_lane == r, x_g, 0.0).astype(BF16))
        y_diag = jnp.dot(jnp.concatenate(mats, axis=1), jnp.concatenate(blocks, axis=0),
                         preferred_element_type=F32)

        st = st_ref[g]
        y_off = jnp.dot(c_g, st.astype(BF16), preferred_element_type=F32) * ef_x[:, ch]
        y = y_diag + y_off + yoffb_ref[:, ch] + dskip_ref[:, ch] * x_g

        xw = (x_g * wf_x[:, ch]).astype(BF16)
        upd = lax.dot_general(b_g, xw, (((0,), (0,)), ((), ())), preferred_element_type=F32)
        st_ref[g] = st * dec_x[:, ch] + upd

        y = y * _silu(z_ref[:, ch].astype(F32))
        ms = jnp.mean(y * y, axis=-1, keepdims=True)
        y_ref[:, ch] = (y * lax.rsqrt(ms + RMS_EPS) * normw_ref[:, ch]).astype(BF16)


def _ssd_fwd(act, p_rest, dtq, yoffb, alog_q, dskip_x, norm_w, e3f, batch, seq):
    m = act.shape[0]
    nc = seq // CHUNK
    cur = lambda b, s: (b * nc + s, 0)
    const = lambda b, s: (0, 0)
    return pl.pallas_call(
        _ssd_fwd_body,
        grid=(batch, nc),
        in_specs=[pl.BlockSpec((CHUNK, D_XBC), cur),
                  pl.BlockSpec((CHUNK, D_SSM), lambda b, s: (b * nc + s, COL_Z)),
                  pl.BlockSpec((CHUNK, DT_PAD), cur),
                  pl.BlockSpec((CHUNK, D_SSM), cur),
                  pl.BlockSpec((1, DT_PAD), const),
                  pl.BlockSpec((1, D_SSM), const),
                  pl.BlockSpec((1, D_SSM), const),
                  pl.BlockSpec((3 * DT_PAD, D_SSM), const)],
        out_specs=pl.BlockSpec((CHUNK, D_SSM), cur),
        out_shape=jax.ShapeDtypeStruct((m, D_SSM), BF16),
        scratch_shapes=[pltpu.VMEM((N_GROUPS, D_STATE, GROUP_W), F32)],
        compiler_params=_cparams(("arbitrary", "arbitrary")),
        name="ssd_fwd",
    )(act, p_rest, dtq, yoffb, alog_q, dskip_x, norm_w, e3f)


def _shortconv_body(h_ref, b_ref, c_ref, hp_ref, cp_ref, hn_ref, cn_ref, cw_ref, nw_ref, y_ref,
                    *, tiles_per_seq, rows):
    i = pl.program_id(0)
    t = i % tiles_per_seq
    has_prev = t > 0
    has_next = t < tiles_per_seq - 1
    lane_tile = 512
    row = lax.broadcasted_iota(jnp.int32, (rows, lane_tile), 0)
    group = D_SC // SC_GROUPS
    for q in range(D_SC // lane_tile):
        sl = slice(q * lane_tile, (q + 1) * lane_tile)
        v = c_ref[:, sl].astype(F32) * h_ref[:, sl].astype(F32)
        last = HALO_BF16 - 1
        vp = jnp.where(has_prev, cp_ref[:, sl].astype(F32)[last:, :] * hp_ref[:, sl].astype(F32)[last:, :], 0.0)
        vn = jnp.where(has_next, cn_ref[:, sl].astype(F32)[0:1, :] * hn_ref[:, sl].astype(F32)[0:1, :], 0.0)
        down = jnp.where(row == 0, vp, pltpu.roll(v, 1, 0))
        up = jnp.where(row == rows - 1, vn, pltpu.roll(v, rows - 1, 0))
        w = cw_ref[:, sl]
        y = b_ref[:, sl].astype(F32) * (w[0:1, :] * down + w[1:2, :] * v + w[2:3, :] * up)
        for k in range(lane_tile // group):
            gs = slice(k * group, (k + 1) * group)
            yg = y[:, gs]
            ms = jnp.mean(yg * yg, axis=-1, keepdims=True)
            lo = q * lane_tile + k * group
            y_ref[:, lo:lo + group] = (yg * lax.rsqrt(ms + RMS_EPS) * nw_ref[:, lo:lo + group]).astype(BF16)


def _shortconv(p_rest, conv_w, norm_w, seq):
    m = p_rest.shape[0]
    rows = 256
    tiles_per_seq = seq // rows
    rb = rows // HALO_BF16
    last_halo = m // HALO_BF16 - 1

    def cur(col):
        return lambda i: (i, col)

    def prev(col):
        return lambda i: (jnp.maximum(i * rb - 1, 0), col)

    def nxt(col):
        return lambda i: (jnp.minimum((i + 1) * rb, last_halo), col)

    const = lambda i: (0, 0)
    return pl.pallas_call(
        functools.partial(_shortconv_body, tiles_per_seq=tiles_per_seq, rows=rows),
        grid=(m // rows,),
        in_specs=[pl.BlockSpec((rows, D_SC), cur(COL_H)),
                  pl.BlockSpec((rows, D_SC), cur(COL_B)),
                  pl.BlockSpec((rows, D_SC), cur(COL_C)),
                  pl.BlockSpec((HALO_BF16, D_SC), prev(COL_H)),
                  pl.BlockSpec((HALO_BF16, D_SC), prev(COL_C)),
                  pl.BlockSpec((HALO_BF16, D_SC), nxt(COL_H)),
                  pl.BlockSpec((HALO_BF16, D_SC), nxt(COL_C)),
                  pl.BlockSpec((HALO, D_SC), const),
                  pl.BlockSpec((1, D_SC), const)],
        out_specs=pl.BlockSpec((rows, D_SC), lambda i: (i, 0)),
        out_shape=jax.ShapeDtypeStruct((m, D_SC), BF16),
        compiler_params=_cparams(("arbitrary",)),
        name="shortconv",
    )(p_rest, p_rest, p_rest, p_rest, p_rest, p_rest, p_rest, conv_w, norm_w)


def _residual_layer_norm(o_ref, x_ref, gate_ref, lng_ref, lnb_ref, alpha, modulated=None, on_slab=None):
    width = o_ref.shape[1]
    lane_tiles = [slice(t, t + LANES) for t in range(0, width, LANES)]

    def slab(i, carry):
        sl = pl.ds(pl.multiple_of(i * LN_ROWS, LN_ROWS), LN_ROWS)
        acc = jnp.zeros((LN_ROWS, LANES), F32)
        for lt in lane_tiles:
            x_lt = slice(lt.start + X_SKEW, lt.stop + X_SKEW)
            r = alpha * x_ref[sl, x_lt] + (1.0 + gate_ref[0, :, lt]) * o_ref[sl, lt]
            o_ref[sl, lt] = r
            acc = acc + r
        mean = jnp.broadcast_to(jnp.sum(acc, axis=-1, keepdims=True) * (1.0 / width), (LN_ROWS, LANES))
        acc = jnp.zeros((LN_ROWS, LANES), F32)
        for lt in lane_tiles:
            d = o_ref[sl, lt] - mean
            acc = acc + d * d
        var = jnp.sum(acc, axis=-1, keepdims=True) * (1.0 / width)
        rstd = jnp.broadcast_to(lax.rsqrt(var + LN_EPS), (LN_ROWS, LANES))
        for lt in lane_tiles:
            y = (o_ref[sl, lt] - mean) * rstd * lng_ref[:, lt] + lnb_ref[:, lt]
            o_ref[sl, lt] = y
            if modulated is not None:
                h_ref, sc_ref, sh_ref = modulated
                h_ref[sl, lt] = (y * (1.0 + sc_ref[0, :, lt]) + sh_ref[0, :, lt]).astype(BF16)
        if on_slab is not None:
            on_slab(i)
        return carry

    lax.fori_loop(0, o_ref.shape[0] // LN_ROWS, slab, 0)


def _row_tile_copy(hbm_ref, buf_ref, sem, tm):
    rows = pl.ds(pl.multiple_of(pl.program_id(0) * tm, tm), tm)
    width = hbm_ref.shape[1]
    return pltpu.make_async_copy(hbm_ref.at[rows, :], buf_ref.at[:, X_SKEW:X_SKEW + width], sem)


def _outproj_body(ys_ref, yc_ref, w_ref, x_hbm, gate_ref, sc_ref, sh_ref, lng_ref, lnb_ref, x1_ref, h2_ref,
                  x_buf, sem, *, alpha, tm, tn):
    n = pl.program_id(1)

    @pl.when(n == 0)
    def _():
        _row_tile_copy(x_hbm, x_buf, sem, tm).start()

    k_ssm = ys_ref.shape[1]
    cols = pl.ds(pl.multiple_of(n * tn, tn), tn)
    x1_ref[:, cols] = (jnp.dot(ys_ref[...], w_ref[:k_ssm, :], preferred_element_type=F32)
                       + jnp.dot(yc_ref[...], w_ref[k_ssm:, :], preferred_element_type=F32))

    @pl.when(n == pl.num_programs(1) - 1)
    def _():
        _row_tile_copy(x_hbm, x_buf, sem, tm).wait()
        _residual_layer_norm(x1_ref, x_buf, gate_ref, lng_ref, lnb_ref, alpha,
                             modulated=(h2_ref, sc_ref, sh_ref))


def _outproj(y_ssm, y_sc, w_out, x2, gate, scale2, shift2, ln_g, ln_b, seq, alpha):
    m, d = x2.shape
    tm, tn = 512, 1024
    tiles_per_batch = seq // tm
    mod_map = lambda i, n: (i // tiles_per_batch, 0, 0)
    const = lambda i, n: (0, 0)
    row_tile = lambda i, n: (i, 0)
    return pl.pallas_call(
        functools.partial(_outproj_body, alpha=alpha, tm=tm, tn=tn),
        grid=(m // tm, d // tn),
        in_specs=[pl.BlockSpec((tm, D_SSM), row_tile),
                  pl.BlockSpec((tm, D_SC), row_tile),
                  pl.BlockSpec((D_SSM + D_SC, tn), lambda i, n: (0, n)),
                  pl.BlockSpec(memory_space=pl.ANY),
                  pl.BlockSpec((1, 1, d), mod_map),
                  pl.BlockSpec((1, 1, d), mod_map),
                  pl.BlockSpec((1, 1, d), mod_map),
                  pl.BlockSpec((1, d), const),
                  pl.BlockSpec((1, d), const)],
        out_specs=[pl.BlockSpec((tm, d), row_tile),
                   pl.BlockSpec((tm, d), row_tile)],
        out_shape=[jax.ShapeDtypeStruct((m, d), F32),
                   jax.ShapeDtypeStruct((m, d), BF16)],
        scratch_shapes=[pltpu.VMEM((tm, d + X_PAD), F32), pltpu.SemaphoreType.DMA(())],
        compiler_params=_cparams(("arbitrary", "arbitrary")),
        name="outproj",
    )(y_ssm, y_sc, w_out, x2, gate, scale2, shift2, ln_g, ln_b)


def _slab_writeback(acc_ref, out_hbm, sems, tm, i):
    src = pl.ds(pl.multiple_of(i * LN_ROWS, LN_ROWS), LN_ROWS)
    dst = pl.ds(pl.multiple_of(pl.program_id(0) * tm + i * LN_ROWS, LN_ROWS), LN_ROWS)
    return pltpu.make_async_copy(acc_ref.at[src, :], out_hbm.at[dst, :], sems.at[i])


def _ffn_body(h2_ref, x1_hbm, gate_ref, wu_ref, wd_ref, lng_ref, lnb_ref, out_hbm, acc_ref, x1_buf, sem,
              out_sems, *, alpha, tm):
    f = pl.program_id(1)

    @pl.when(f == 0)
    def _():
        _row_tile_copy(x1_hbm, x1_buf, sem, tm).start()
        acc_ref[...] = jnp.zeros_like(acc_ref)

    u = jnp.maximum(jnp.dot(h2_ref[...], wu_ref[...], preferred_element_type=F32), 0.0)
    u = (u * u).astype(BF16)
    for n0 in range(0, acc_ref.shape[1], ACC_COLS):
        cols = slice(n0, n0 + ACC_COLS)
        acc_ref[:, cols] += jnp.dot(u, wd_ref[:, cols], preferred_element_type=F32)

    @pl.when(f == pl.num_programs(1) - 1)
    def _():
        _row_tile_copy(x1_hbm, x1_buf, sem, tm).wait()
        _residual_layer_norm(acc_ref, x1_buf, gate_ref, lng_ref, lnb_ref, alpha,
                             on_slab=lambda i: _slab_writeback(acc_ref, out_hbm, out_sems, tm, i).start())
        for i in range(tm // LN_ROWS):
            _slab_writeback(acc_ref, out_hbm, out_sems, tm, i).wait()


def _ffn(h2, x1, gate, w_up, w_down, ln_g, ln_b, seq, alpha):
    m, d = x1.shape
    ff = w_up.shape[1]
    tm, tf = 512, 1024
    tiles_per_batch = seq // tm
    mod_map = lambda i, f: (i // tiles_per_batch, 0, 0)
    const = lambda i, f: (0, 0)
    row_tile = lambda i, f: (i, 0)
    return pl.pallas_call(
        functools.partial(_ffn_body, alpha=alpha, tm=tm),
        grid=(m // tm, ff // tf),
        in_specs=[pl.BlockSpec((tm, d), row_tile),
                  pl.BlockSpec(memory_space=pl.ANY),
                  pl.BlockSpec((1, 1, d), mod_map),
                  pl.BlockSpec((d, tf), lambda i, f: (0, f)),
                  pl.BlockSpec((tf, d), lambda i, f: (f, 0)),
                  pl.BlockSpec((1, d), const),
                  pl.BlockSpec((1, d), const)],
        out_specs=pl.BlockSpec(memory_space=pl.ANY),
        out_shape=jax.ShapeDtypeStruct((m, d), F32),
        scratch_shapes=[pltpu.VMEM((tm, d), F32), pltpu.VMEM((tm, d + X_PAD), F32), pltpu.SemaphoreType.DMA(()),
                        pltpu.SemaphoreType.DMA((tm // LN_ROWS,))],
        compiler_params=_cparams(("arbitrary", "arbitrary")),
        name="ffn",
    )(h2, x1, gate, w_up, w_down, ln_g, ln_b)


def _head_expansion(lane_offset):
    rows = jnp.arange(DT_PAD)[:, None]
    heads = jnp.arange(D_SSM)[None, :] // HEAD_DIM
    e = (rows == heads + lane_offset).astype(BF16)
    return jnp.tile(e, (3, 1))


def _pad_rows(a, rows):
    return jnp.pad(a, ((0, rows - a.shape[0]), (0, 0)))


def kernel(x, c, w_ada, b_ada, w_in, ssm_conv_w, ssm_conv_b, ssm_dt_bias_f, ssm_dt_bias_b, ssm_a_log_f, ssm_a_log_b, ssm_d, ssm_norm_w, sc_conv_w, sc_norm_w, w_out, ln1_g, ln1_b, w_up, w_down, ln2_g, ln2_b):
    batch, seq, d = x.shape
    depth = w_ada.shape[0]
    alpha = float((2 * depth) ** 0.25)
    m = batch * seq
    dt_lo = D_SSM + D_XBC
    dt_hi = dt_lo + 2 * N_HEADS
    e3f = _head_expansion(0)
    e3b = _head_expansion(N_HEADS)
    c_pad = _pad_rows(c, HALO)
    x2 = x.reshape(m, d)

    for l in range(depth):
        mod = _ada(c_pad, w_ada[l], b_ada[l][None, :])[:batch]
        shift1, scale1, gate1, shift2, scale2, gate2 = [
            t[:, None, :] for t in jnp.split(mod, N_MOD, axis=-1)]

        w_t = w_in[l].T.astype(BF16)
        w_dt_t = jnp.pad(w_t[dt_lo:dt_hi], ((0, DT_PAD - 2 * N_HEADS), (0, 0)))
        p_xbc, p_rest, dt = _inproj(x2, scale1, shift1, w_t, w_dt_t, seq)

        zeros_q = jnp.zeros((DT_PAD - 2 * N_HEADS,), F32)
        bias_q = jnp.concatenate([ssm_dt_bias_f[l], ssm_dt_bias_b[l], zeros_q])[None, :]
        alog_q = jnp.concatenate([ssm_a_log_f[l], ssm_a_log_b[l], zeros_q])[None, :]
        act, yoffb, dtq = _ssd_bwd(p_xbc, dt, _pad_rows(ssm_conv_w[l], HALO), ssm_conv_b[l][None, :],
                                   bias_q, alog_q, e3b, batch, seq)
        dskip_x = jnp.repeat(ssm_d[l], HEAD_DIM)[None, :]
        y_ssm = _ssd_fwd(act, p_rest, dtq, yoffb, alog_q, dskip_x, ssm_norm_w[l][None, :], e3f, batch, seq)
        y_sc = _shortconv(p_rest, _pad_rows(sc_conv_w[l], HALO), sc_norm_w[l][None, :], seq)

        x1, h2 = _outproj(y_ssm, y_sc, w_out[l].astype(BF16), x2, gate1, scale2, shift2,
                          ln1_g[l][None, :], ln1_b[l][None, :], seq, alpha)
        x2 = _ffn(h2, x1, gate2, w_up[l].astype(BF16), w_down[l].astype(BF16),
                  ln2_g[l][None, :], ln2_b[l][None, :], seq, alpha)
    return x2.reshape(batch, seq, d)
```

```python
import functools

import jax
import jax.numpy as jnp
from jax import lax
from jax.experimental import pallas as pl
from jax.experimental.pallas import tpu as pltpu

F32 = jnp.float32
BF16 = jnp.bfloat16
HIGHEST = lax.Precision.HIGHEST

D_MODEL = 4096
D_SSM = 2048
D_SC = 2048
HEAD_DIM = 64
N_HEADS = 32
N_GROUPS = 8
HEADS_PER_GROUP = 4
D_STATE = 128
CHUNK = 128
SSM_CONV = 5
SC_CONV = 3
SC_GROUPS = 16
D_XBC = D_SSM + 2 * N_GROUPS * D_STATE
D_FF = 4 * D_MODEL
N_MOD = 6
LN_EPS = 1e-5
RMS_EPS = 1e-5
GROUP_W = HEADS_PER_GROUP * HEAD_DIM
D_MAIN = D_XBC + D_SSM + 3 * D_SC
DT_PAD = 128
HALO = 8
HALO_BF16 = 16
LANES = 128
LN_ROWS = 64
ACC_COLS = 512

COL_Z = 0
COL_H = 1
COL_B = 2
COL_C = 3

VMEM_LIMIT = 62 * 1024 * 1024


def _cparams(sem):
    return pltpu.CompilerParams(dimension_semantics=sem, vmem_limit_bytes=VMEM_LIMIT)


def _sigmoid(v):
    return 1.0 / (1.0 + jnp.exp(-v))


def _silu(v):
    return v * _sigmoid(v)


def _softplus(v):
    return jnp.maximum(v, 0.0) + jnp.log1p(jnp.exp(-jnp.abs(v)))


def _layer_norm(r, g, b):
    mu = jnp.mean(r, axis=-1, keepdims=True)
    d = r - mu
    var = jnp.mean(d * d, axis=-1, keepdims=True)
    return d * lax.rsqrt(var + LN_EPS) * g + b


def _ada_body(c_ref, w_ref, b_ref, o_ref):
    o_ref[...] = jnp.dot(_silu(c_ref[...]).astype(BF16), w_ref[...].astype(BF16),
                         preferred_element_type=F32) + b_ref[...]


def _ada(c_pad, w_ada, b_ada):
    rows, d = c_pad.shape
    n = w_ada.shape[1]
    tn = 512
    return pl.pallas_call(
        _ada_body,
        grid=(n // tn,),
        in_specs=[pl.BlockSpec((rows, d), lambda j: (0, 0)),
                  pl.BlockSpec((d, tn), lambda j: (0, j)),
                  pl.BlockSpec((1, tn), lambda j: (0, j))],
        out_specs=pl.BlockSpec((rows, tn), lambda j: (0, j)),
        out_shape=jax.ShapeDtypeStruct((rows, n), F32),
        compiler_params=_cparams(("arbitrary",)),
        name="ada",
    )(c_pad, w_ada, b_ada)


def _dot_nt(a, b_t):
    return lax.dot_general(a, b_t, (((1,), (1,)), ((), ())), preferred_element_type=F32)


def _inproj_body(x_ref, sc_ref, sh_ref, w_ref, wdt_ref, xbc_ref, rest_ref, dt_ref, h_scr, *, xbc_tiles):
    j = pl.program_id(1)

    @pl.when(j == 0)
    def _():
        h = (x_ref[...] * (1.0 + sc_ref[0]) + sh_ref[0]).astype(BF16)
        h_scr[...] = h
        dt_ref[...] = _dot_nt(h, wdt_ref[...])

    @pl.when(j < xbc_tiles)
    def _():
        xbc_ref[...] = _dot_nt(h_scr[...], w_ref[...])

    @pl.when(j >= xbc_tiles)
    def _():
        rest_ref[...] = _dot_nt(h_scr[...], w_ref[...]).astype(BF16)


def _inproj(x2, scale, shift, w_t, w_dt_t, seq):
    m, d = x2.shape
    n = D_MAIN
    tm, tn = 1024, 512
    tiles_per_batch = seq // tm
    xbc_tiles = D_XBC // tn
    z_tiles = D_SSM // tn
    hbc_start = D_SSM + D_XBC + 2 * N_HEADS

    def w_row_start(j):
        start = jnp.where(j < xbc_tiles, D_SSM + j * tn,
                          jnp.where(j < xbc_tiles + z_tiles, (j - xbc_tiles) * tn,
                                    hbc_start + (j - xbc_tiles - z_tiles) * tn))
        return pl.multiple_of(start, 2 * N_HEADS)
    return pl.pallas_call(
        functools.partial(_inproj_body, xbc_tiles=xbc_tiles),
        grid=(m // tm, n // tn),
        in_specs=[pl.BlockSpec((tm, d), lambda i, j: (i, 0)),
                  pl.BlockSpec((1, 1, d), lambda i, j: (i // tiles_per_batch, 0, 0)),
                  pl.BlockSpec((1, 1, d), lambda i, j: (i // tiles_per_batch, 0, 0)),
                  pl.BlockSpec((pl.Element(tn), pl.Element(d)), lambda i, j: (w_row_start(j), 0)),
                  pl.BlockSpec((DT_PAD, d), lambda i, j: (0, 0))],
        out_specs=[pl.BlockSpec((tm, tn), lambda i, j: (i, jnp.minimum(j, xbc_tiles - 1))),
                   pl.BlockSpec((tm, tn), lambda i, j: (i, jnp.maximum(j - xbc_tiles, 0))),
                   pl.BlockSpec((tm, DT_PAD), lambda i, j: (i, 0))],
        out_shape=[jax.ShapeDtypeStruct((m, D_XBC), F32),
                   jax.ShapeDtypeStruct((m, n - D_XBC), BF16),
                   jax.ShapeDtypeStruct((m, DT_PAD), F32)],
        scratch_shapes=[pltpu.VMEM((tm, d), BF16)],
        compiler_params=_cparams(("arbitrary", "arbitrary")),
        name="inproj",
    )(x2, scale, shift, w_t, w_dt_t)


def _expand_heads(q, e3_ref):
    hi = q.astype(BF16)
    r1 = q - hi.astype(F32)
    mid = r1.astype(BF16)
    lo = (r1 - mid.astype(F32)).astype(BF16)
    pieces = jnp.concatenate([hi, mid, lo], axis=1)
    return jnp.dot(pieces, e3_ref[...], preferred_element_type=F32)


def _chunk_decay_terms(dtq, alog_ref):
    row = lax.broadcasted_iota(jnp.int32, (CHUNK, CHUNK), 0)
    col = lax.broadcasted_iota(jnp.int32, (CHUNK, CHUNK), 1)
    tril = col <= row
    triu = col >= row
    da = dtq * (-jnp.exp(alog_ref[...]))
    pre = jnp.dot(tril.astype(F32), da, preferred_element_type=F32, precision=HIGHEST)
    suf = jnp.dot(triu.astype(F32), da, preferred_element_type=F32, precision=HIGHEST)
    acum = jnp.where(col < N_HEADS, pre, suf)
    return acum, tril, triu


def _ssd_bwd_body(cur_ref, prev_ref, next_ref, dt_ref, cw_ref, cb_ref, bias_ref, alog_ref, e3_ref,
                  act_ref, yoff_ref, dtq_ref, st_ref, *, n_chunks):
    s = pl.program_id(1)
    c = n_chunks - 1 - s

    @pl.when(s == 0)
    def _():
        st_ref[...] = jnp.zeros_like(st_ref)

    lane_tile = 512
    rows = CHUNK + 2 * HALO
    has_prev = c > 0
    has_next = c < n_chunks - 1
    for t in range(D_XBC // lane_tile):
        sl = slice(t * lane_tile, (t + 1) * lane_tile)
        stack = jnp.concatenate([jnp.where(has_prev, prev_ref[:, sl], 0.0), cur_ref[:, sl],
                                 jnp.where(has_next, next_ref[:, sl], 0.0)], axis=0)
        w = cw_ref[:, sl]
        conv = cb_ref[:, sl] + w[SSM_CONV // 2:SSM_CONV // 2 + 1, :] * stack[HALO:HALO + CHUNK, :]
        for k in range(SSM_CONV):
            shift = SSM_CONV // 2 - k
            if shift != 0:
                tap = pltpu.roll(stack, shift % rows, 0)[HALO:HALO + CHUNK, :]
                conv = conv + w[k:k + 1, :] * tap
        act_ref[:, sl] = _silu(conv)

    dtq = _softplus(dt_ref[...] + bias_ref[...])
    dtq_ref[...] = dtq
    acum, _, _ = _chunk_decay_terms(dtq, alog_ref)
    lane = lax.broadcasted_iota(jnp.int32, (CHUNK, DT_PAD), 1)
    acum = jnp.where((lane >= N_HEADS) & (lane < 2 * N_HEADS), acum, 0.0)
    total = acum[0:1, :]
    eb_x = _expand_heads(jnp.exp(acum), e3_ref)
    wb_x = _expand_heads(dtq * jnp.exp(total - acum), e3_ref)
    dec_x = eb_x[0:1, :]

    for g in range(N_GROUPS):
        ch = slice(g * GROUP_W, (g + 1) * GROUP_W)
        b_g = act_ref[:, D_SSM + g * D_STATE:D_SSM + (g + 1) * D_STATE].astype(BF16)
        c_g = act_ref[:, D_SSM + N_GROUPS * D_STATE + g * D_STATE:
                      D_SSM + N_GROUPS * D_STATE + (g + 1) * D_STATE].astype(BF16)
        st = st_ref[g]
        yoff_ref[:, ch] = jnp.dot(c_g, st.astype(BF16), preferred_element_type=F32) * eb_x[:, ch]
        xw = (act_ref[:, ch] * wb_x[:, ch]).astype(BF16)
        upd = lax.dot_general(b_g, xw, (((0,), (0,)), ((), ())), preferred_element_type=F32)
        st_ref[g] = st * dec_x[:, ch] + upd


def _ssd_bwd(p_main, dt, conv_w, conv_b, bias_q, alog_q, e3b, batch, seq):
    m = p_main.shape[0]
    nc = seq // CHUNK
    rb = CHUNK // HALO
    last_halo = m // HALO - 1

    def cur_map(b, s):
        return (b * nc + (nc - 1 - s), 0)

    def prev_map(b, s):
        return (jnp.maximum((b * nc + (nc - 1 - s)) * rb - 1, 0), 0)

    def next_map(b, s):
        return (jnp.minimum((b * nc + (nc - 1 - s) + 1) * rb, last_halo), 0)

    const = lambda b, s: (0, 0)
    return pl.pallas_call(
        functools.partial(_ssd_bwd_body, n_chunks=nc),
        grid=(batch, nc),
        in_specs=[pl.BlockSpec((CHUNK, D_XBC), cur_map),
                  pl.BlockSpec((HALO, D_XBC), prev_map),
                  pl.BlockSpec((HALO, D_XBC), next_map),
                  pl.BlockSpec((CHUNK, DT_PAD), cur_map),
                  pl.BlockSpec((HALO, D_XBC), const),
                  pl.BlockSpec((1, D_XBC), const),
                  pl.BlockSpec((1, DT_PAD), const),
                  pl.BlockSpec((1, DT_PAD), const),
                  pl.BlockSpec((3 * DT_PAD, D_SSM), const)],
        out_specs=[pl.BlockSpec((CHUNK, D_XBC), cur_map),
                   pl.BlockSpec((CHUNK, D_SSM), cur_map),
                   pl.BlockSpec((CHUNK, DT_PAD), cur_map)],
        out_shape=[jax.ShapeDtypeStruct((m, D_XBC), F32),
                   jax.ShapeDtypeStruct((m, D_SSM), F32),
                   jax.ShapeDtypeStruct((m, DT_PAD), F32)],
        scratch_shapes=[pltpu.VMEM((N_GROUPS, D_STATE, GROUP_W), F32)],
        compiler_params=_cparams(("arbitrary", "arbitrary")),
        name="ssd_bwd",
    )(p_main, p_main, p_main, dt, conv_w, conv_b, bias_q, alog_q, e3b)


def _ssd_fwd_body(act_ref, z_ref, dtq_ref, yoffb_ref, alog_ref, dskip_ref, normw_ref, e3_ref,
                  y_ref, st_ref):
    @pl.when(pl.program_id(1) == 0)
    def _():
        st_ref[...] = jnp.zeros_like(st_ref)

    dtq = dtq_ref[...]
    acum, tril, triu = _chunk_decay_terms(dtq, alog_ref)
    acum_t = acum.T
    dtq_t = dtq.T
    lane = lax.broadcasted_iota(jnp.int32, (CHUNK, DT_PAD), 1)
    acum_f = jnp.where(lane < N_HEADS, acum, 0.0)
    total = acum_f[CHUNK - 1:CHUNK, :]
    ef_x = _expand_heads(jnp.exp(acum_f), e3_ref)
    wf_x = _expand_heads(dtq * jnp.exp(total - acum_f), e3_ref)
    dec_x = ef_x[CHUNK - 1:CHUNK, :]
    head_of_lane = lax.broadcasted_iota(jnp.int32, (CHUNK, GROUP_W), 1) // HEAD_DIM
    neg_inf = jnp.float32(-jnp.inf)

    for g in range(N_GROUPS):
        ch = slice(g * GROUP_W, (g + 1) * GROUP_W)
        b_g = act_ref[:, D_SSM + g * D_STATE:D_SSM + (g + 1) * D_STATE].astype(BF16)
        c_g = act_ref[:, D_SSM + N_GROUPS * D_STATE + g * D_STATE:
                      D_SSM + N_GROUPS * D_STATE + (g + 1) * D_STATE].astype(BF16)
        x_g = act_ref[:, ch]
        scores = lax.dot_general(c_g, b_g, (((1,), (1,)), ((), ())), preferred_element_type=F32)

        mats = []
        blocks = []
        for r in range(HEADS_PER_GROUP):
            hf = g * HEADS_PER_GROUP + r
            hb = N_HEADS + hf
            seg_f = acum[:, hf:hf + 1] - acum_t[hf:hf + 1, :]
            seg_b = acum[:, hb:hb + 1] - acum_t[hb:hb + 1, :]
            lf = jnp.exp(jnp.where(tril, seg_f, neg_inf)) * dtq_t[hf:hf + 1, :]
            ub = jnp.exp(jnp.where(triu, seg_b, neg_inf)) * dtq_t[hb:hb + 1, :]
            mats.append((scores * (lf + ub)).astype(BF16))
            blocks.append(jnp.where(head_of_lane == r, x_g, 0.0).astype(BF16))
        y_diag = jnp.dot(jnp.concatenate(mats, axis=1), jnp.concatenate(blocks, axis=0),
                         preferred_element_type=F32)

        st = st_ref[g]
        y_off = jnp.dot(c_g, st.astype(BF16), preferred_element_type=F32) * ef_x[:, ch]
        y = y_diag + y_off + yoffb_ref[:, ch] + dskip_ref[:, ch] * x_g

        xw = (x_g * wf_x[:, ch]).astype(BF16)
        upd = lax.dot_general(b_g, xw, (((0,), (0,)), ((), ())), preferred_element_type=F32)
        st_ref[g] = st * dec_x[:, ch] + upd

        y = y * _silu(z_ref[:, ch].astype(F32))
        ms = jnp.mean(y * y, axis=-1, keepdims=True)
        y_ref[:, ch] = (y * lax.rsqrt(ms + RMS_EPS) * normw_ref[:, ch]).astype(BF16)


def _ssd_fwd(act, p_rest, dtq, yoffb, alog_q, dskip_x, norm_w, e3f, batch, seq):
    m = act.shape[0]
    nc = seq // CHUNK
    cur = lambda b, s: (b * nc + s, 0)
    const = lambda b, s: (0, 0)
    return pl.pallas_call(
        _ssd_fwd_body,
        grid=(batch, nc),
        in_specs=[pl.BlockSpec((CHUNK, D_XBC), cur),
                  pl.BlockSpec((CHUNK, D_SSM), lambda b, s: (b * nc + s, COL_Z)),
                  pl.BlockSpec((CHUNK, DT_PAD), cur),
                  pl.BlockSpec((CHUNK, D_SSM), cur),
                  pl.BlockSpec((1, DT_PAD), const),
                  pl.BlockSpec((1, D_SSM), const),
                  pl.BlockSpec((1, D_SSM), const),
                  pl.BlockSpec((3 * DT_PAD, D_SSM), const)],
        out_specs=pl.BlockSpec((CHUNK, D_SSM), cur),
        out_shape=jax.ShapeDtypeStruct((m, D_SSM), BF16),
        scratch_shapes=[pltpu.VMEM((N_GROUPS, D_STATE, GROUP_W), F32)],
        compiler_params=_cparams(("arbitrary", "arbitrary")),
        name="ssd_fwd",
    )(act, p_rest, dtq, yoffb, alog_q, dskip_x, norm_w, e3f)


def _shortconv_body(h_ref, b_ref, c_ref, hp_ref, cp_ref, hn_ref, cn_ref, cw_ref, nw_ref, y_ref,
                    *, tiles_per_seq, rows):
    i = pl.program_id(0)
    t = i % tiles_per_seq
    has_prev = t > 0
    has_next = t < tiles_per_seq - 1
    lane_tile = 512
    row = lax.broadcasted_iota(jnp.int32, (rows, lane_tile), 0)
    group = D_SC // SC_GROUPS
    for q in range(D_SC // lane_tile):
        sl = slice(q * lane_tile, (q + 1) * lane_tile)
        v = c_ref[:, sl].astype(F32) * h_ref[:, sl].astype(F32)
        last = HALO_BF16 - 1
        vp = jnp.where(has_prev, cp_ref[:, sl].astype(F32)[last:, :] * hp_ref[:, sl].astype(F32)[last:, :], 0.0)
        vn = jnp.where(has_next, cn_ref[:, sl].astype(F32)[0:1, :] * hn_ref[:, sl].astype(F32)[0:1, :], 0.0)
        down = jnp.where(row == 0, vp, pltpu.roll(v, 1, 0))
        up = jnp.where(row == rows - 1, vn, pltpu.roll(v, rows - 1, 0))
        w = cw_ref[:, sl]
        y = b_ref[:, sl].astype(F32) * (w[0:1, :] * down + w[1:2, :] * v + w[2:3, :] * up)
        for k in range(lane_tile // group):
            gs = slice(k * group, (k + 1) * group)
            yg = y[:, gs]
            ms = jnp.mean(yg * yg, axis=-1, keepdims=True)
            lo = q * lane_tile + k * group
            y_ref[:, lo:lo + group] = (yg * lax.rsqrt(ms + RMS_EPS) * nw_ref[:, lo:lo + group]).astype(BF16)


def _shortconv(p_rest, conv_w, norm_w, seq):
    m = p_rest.shape[0]
    rows = 256
    tiles_per_seq = seq // rows
    rb = rows // HALO_BF16
    last_halo = m // HALO_BF16 - 1

    def cur(col):
        return lambda i: (i, col)

    def prev(col):
        return lambda i: (jnp.maximum(i * rb - 1, 0), col)

    def nxt(col):
        return lambda i: (jnp.minimum((i + 1) * rb, last_halo), col)

    const = lambda i: (0, 0)
    return pl.pallas_call(
        functools.partial(_shortconv_body, tiles_per_seq=tiles_per_seq, rows=rows),
        grid=(m // rows,),
        in_specs=[pl.BlockSpec((rows, D_SC), cur(COL_H)),
                  pl.BlockSpec((rows, D_SC), cur(COL_B)),
                  pl.BlockSpec((rows, D_SC), cur(COL_C)),
                  pl.BlockSpec((HALO_BF16, D_SC), prev(COL_H)),
                  pl.BlockSpec((HALO_BF16, D_SC), prev(COL_C)),
                  pl.BlockSpec((HALO_BF16, D_SC), nxt(COL_H)),
                  pl.BlockSpec((HALO_BF16, D_SC), nxt(COL_C)),
                  pl.BlockSpec((HALO, D_SC), const),
                  pl.BlockSpec((1, D_SC), const)],
        out_specs=pl.BlockSpec((rows, D_SC), lambda i: (i, 0)),
        out_shape=jax.ShapeDtypeStruct((m, D_SC), BF16),
        compiler_params=_cparams(("arbitrary",)),
        name="shortconv",
    )(p_rest, p_rest, p_rest, p_rest, p_rest, p_rest, p_rest, conv_w, norm_w)


def _residual_layer_norm(o_ref, x_ref, gate_ref, lng_ref, lnb_ref, alpha, modulated=None, on_slab=None):
    width = o_ref.shape[1]
    lane_tiles = [slice(t, t + LANES) for t in range(0, width, LANES)]

    def slab(i, carry):
        sl = pl.ds(pl.multiple_of(i * LN_ROWS, LN_ROWS), LN_ROWS)
        acc = jnp.zeros((LN_ROWS, LANES), F32)
        for lt in lane_tiles:
            r = alpha * x_ref[sl, lt] + (1.0 + gate_ref[0, :, lt]) * o_ref[sl, lt]
            o_ref[sl, lt] = r
            acc = acc + r
        mean = jnp.broadcast_to(jnp.sum(acc, axis=-1, keepdims=True) * (1.0 / width), (LN_ROWS, LANES))
        acc = jnp.zeros((LN_ROWS, LANES), F32)
        for lt in lane_tiles:
            d = o_ref[sl, lt] - mean
            acc = acc + d * d
        var = jnp.sum(acc, axis=-1, keepdims=True) * (1.0 / width)
        rstd = jnp.broadcast_to(lax.rsqrt(var + LN_EPS), (LN_ROWS, LANES))
        for lt in lane_tiles:
            y = (o_ref[sl, lt] - mean) * rstd * lng_ref[:, lt] + lnb_ref[:, lt]
            o_ref[sl, lt] = y
            if modulated is not None:
                h_ref, sc_ref, sh_ref = modulated
                h_ref[sl, lt] = (y * (1.0 + sc_ref[0, :, lt]) + sh_ref[0, :, lt]).astype(BF16)
        if on_slab is not None:
            on_slab(i)
        return carry

    lax.fori_loop(0, o_ref.shape[0] // LN_ROWS, slab, 0)


def _row_tile_copy(hbm_ref, buf_ref, sem, tm):
    rows = pl.ds(pl.multiple_of(pl.program_id(0) * tm, tm), tm)
    return pltpu.make_async_copy(hbm_ref.at[rows, :], buf_ref, sem)


def _slab_writeback(buf_ref, out_hbm, sems, tm, i):
    src = pl.ds(pl.multiple_of(i * LN_ROWS, LN_ROWS), LN_ROWS)
    dst = pl.ds(pl.multiple_of(pl.program_id(0) * tm + i * LN_ROWS, LN_ROWS), LN_ROWS)
    return pltpu.make_async_copy(buf_ref.at[src, :], out_hbm.at[dst, :], sems.at[i])


def _outproj_body(ys_ref, yc_ref, w_ref, x_hbm, gate_ref, sc_ref, sh_ref, lng_ref, lnb_ref, x1_hbm, h2_hbm,
                  x1_buf, h2_buf, x_buf, sem, x1_sems, h2_sems, *, alpha, tm, tn):
    n = pl.program_id(1)

    @pl.when(n == 1)
    def _():
        _row_tile_copy(x_hbm, x_buf, sem, tm).start()

    k_ssm = ys_ref.shape[1]
    cols = pl.ds(pl.multiple_of(n * tn, tn), tn)
    x1_buf[:, cols] = (jnp.dot(ys_ref[...], w_ref[:k_ssm, :], preferred_element_type=F32)
                       + jnp.dot(yc_ref[...], w_ref[k_ssm:, :], preferred_element_type=F32))

    @pl.when(n == pl.num_programs(1) - 1)
    def _():
        _row_tile_copy(x_hbm, x_buf, sem, tm).wait()

        def flush(i):
            _slab_writeback(x1_buf, x1_hbm, x1_sems, tm, i).start()
            _slab_writeback(h2_buf, h2_hbm, h2_sems, tm, i).start()

        _residual_layer_norm(x1_buf, x_buf, gate_ref, lng_ref, lnb_ref, alpha,
                             modulated=(h2_buf, sc_ref, sh_ref), on_slab=flush)
        for i in range(tm // LN_ROWS):
            _slab_writeback(x1_buf, x1_hbm, x1_sems, tm, i).wait()
            _slab_writeback(h2_buf, h2_hbm, h2_sems, tm, i).wait()


def _outproj(y_ssm, y_sc, w_out, x2, gate, scale2, shift2, ln_g, ln_b, seq, alpha):
    m, d = x2.shape
    tm, tn = 512, 1024
    tiles_per_batch = seq // tm
    mod_map = lambda i, n: (i // tiles_per_batch, 0, 0)
    const = lambda i, n: (0, 0)
    row_tile = lambda i, n: (i, 0)
    return pl.pallas_call(
        functools.partial(_outproj_body, alpha=alpha, tm=tm, tn=tn),
        grid=(m // tm, d // tn),
        in_specs=[pl.BlockSpec((tm, D_SSM), row_tile),
                  pl.BlockSpec((tm, D_SC), row_tile),
                  pl.BlockSpec((D_SSM + D_SC, tn), lambda i, n: (0, n)),
                  pl.BlockSpec(memory_space=pl.ANY),
                  pl.BlockSpec((1, 1, d), mod_map),
                  pl.BlockSpec((1, 1, d), mod_map),
                  pl.BlockSpec((1, 1, d), mod_map),
                  pl.BlockSpec((1, d), const),
                  pl.BlockSpec((1, d), const)],
        out_specs=[pl.BlockSpec(memory_space=pl.ANY),
                   pl.BlockSpec(memory_space=pl.ANY)],
        out_shape=[jax.ShapeDtypeStruct((m, d), F32),
                   jax.ShapeDtypeStruct((m, d), BF16)],
        scratch_shapes=[pltpu.VMEM((tm, d), F32), pltpu.VMEM((tm, d), BF16), pltpu.VMEM((tm, d), F32),
                        pltpu.SemaphoreType.DMA(()), pltpu.SemaphoreType.DMA((tm // LN_ROWS,)),
                        pltpu.SemaphoreType.DMA((tm // LN_ROWS,))],
        compiler_params=_cparams(("arbitrary", "arbitrary")),
        name="outproj",
    )(y_ssm, y_sc, w_out, x2, gate, scale2, shift2, ln_g, ln_b)


def _ffn_body(h2_ref, x1_hbm, gate_ref, wu_ref, wd_ref, lng_ref, lnb_ref, out_hbm, acc_ref, x1_buf, sem,
              out_sems, *, alpha, tm):
    f = pl.program_id(1)

    @pl.when(f == 0)
    def _():
        acc_ref[...] = jnp.zeros_like(acc_ref)

    @pl.when(f == 1)
    def _():
        _row_tile_copy(x1_hbm, x1_buf, sem, tm).start()

    u = jnp.maximum(jnp.dot(h2_ref[...], wu_ref[...], preferred_element_type=F32), 0.0)
    u = (u * u).astype(BF16)
    for n0 in range(0, acc_ref.shape[1], ACC_COLS):
        cols = slice(n0, n0 + ACC_COLS)
        acc_ref[:, cols] += jnp.dot(u, wd_ref[:, cols], preferred_element_type=F32)

    @pl.when(f == pl.num_programs(1) - 1)
    def _():
        _row_tile_copy(x1_hbm, x1_buf, sem, tm).wait()
        _residual_layer_norm(acc_ref, x1_buf, gate_ref, lng_ref, lnb_ref, alpha,
                             on_slab=lambda i: _slab_writeback(acc_ref, out_hbm, out_sems, tm, i).start())
        for i in range(tm // LN_ROWS):
            _slab_writeback(acc_ref, out_hbm, out_sems, tm, i).wait()


def _ffn(h2, x1, gate, w_up, w_down, ln_g, ln_b, seq, alpha):
    m, d = x1.shape
    ff = w_up.shape[1]
    tm, tf = 512, 1024
    tiles_per_batch = seq // tm
    mod_map = lambda i, f: (i // tiles_per_batch, 0, 0)
    const = lambda i, f: (0, 0)
    row_tile = lambda i, f: (i, 0)
    return pl.pallas_call(
        functools.partial(_ffn_body, alpha=alpha, tm=tm),
        grid=(m // tm, ff // tf),
        in_specs=[pl.BlockSpec((tm, d), row_tile),
                  pl.BlockSpec(memory_space=pl.ANY),
                  pl.BlockSpec((1, 1, d), mod_map),
                  pl.BlockSpec((d, tf), lambda i, f: (0, f)),
                  pl.BlockSpec((tf, d), lambda i, f: (f, 0)),
                  pl.BlockSpec((1, d), const),
                  pl.BlockSpec((1, d), const)],
        out_specs=pl.BlockSpec(memory_space=pl.ANY),
        out_shape=jax.ShapeDtypeStruct((m, d), F32),
        scratch_shapes=[pltpu.VMEM((tm, d), F32), pltpu.VMEM((tm, d), F32), pltpu.SemaphoreType.DMA(()),
                        pltpu.SemaphoreType.DMA((tm // LN_ROWS,))],
        compiler_params=_cparams(("arbitrary", "arbitrary")),
        name="ffn",
    )(h2, x1, gate, w_up, w_down, ln_g, ln_b)


def _head_expansion(lane_offset):
    rows = jnp.arange(DT_PAD)[:, None]
    heads = jnp.arange(D_SSM)[None, :] // HEAD_DIM
    e = (rows == heads + lane_offset).astype(BF16)
    return jnp.tile(e, (3, 1))


def _pad_rows(a, rows):
    return jnp.pad(a, ((0, rows - a.shape[0]), (0, 0)))


def kernel(x, c, w_ada, b_ada, w_in, ssm_conv_w, ssm_conv_b, ssm_dt_bias_f, ssm_dt_bias_b, ssm_a_log_f, ssm_a_log_b, ssm_d, ssm_norm_w, sc_conv_w, sc_norm_w, w_out, ln1_g, ln1_b, w_up, w_down, ln2_g, ln2_b):
    batch, seq, d = x.shape
    depth = w_ada.shape[0]
    alpha = float((2 * depth) ** 0.25)
    m = batch * seq
    dt_lo = D_SSM + D_XBC
    dt_hi = dt_lo + 2 * N_HEADS
    e3f = _head_expansion(0)
    e3b = _head_expansion(N_HEADS)
    c_pad = _pad_rows(c, HALO)
    x2 = x.reshape(m, d)

    for l in range(depth):
        mod = _ada(c_pad, w_ada[l], b_ada[l][None, :])[:batch]
        shift1, scale1, gate1, shift2, scale2, gate2 = [
            t[:, None, :] for t in jnp.split(mod, N_MOD, axis=-1)]

        w_t = w_in[l].T.astype(BF16)
        w_dt_t = jnp.pad(w_t[dt_lo:dt_hi], ((0, DT_PAD - 2 * N_HEADS), (0, 0)))
        p_xbc, p_rest, dt = _inproj(x2, scale1, shift1, w_t, w_dt_t, seq)

        zeros_q = jnp.zeros((DT_PAD - 2 * N_HEADS,), F32)
        bias_q = jnp.concatenate([ssm_dt_bias_f[l], ssm_dt_bias_b[l], zeros_q])[None, :]
        alog_q = jnp.concatenate([ssm_a_log_f[l], ssm_a_log_b[l], zeros_q])[None, :]
        act, yoffb, dtq = _ssd_bwd(p_xbc, dt, _pad_rows(ssm_conv_w[l], HALO), ssm_conv_b[l][None, :],
                                   bias_q, alog_q, e3b, batch, seq)
        dskip_x = jnp.repeat(ssm_d[l], HEAD_DIM)[None, :]
        y_ssm = _ssd_fwd(act, p_rest, dtq, yoffb, alog_q, dskip_x, ssm_norm_w[l][None, :], e3f, batch, seq)
        y_sc = _shortconv(p_rest, _pad_rows(sc_conv_w[l], HALO), sc_norm_w[l][None, :], seq)

        x1, h2 = _outproj(y_ssm, y_sc, w_out[l].astype(BF16), x2, gate1, scale2, shift2,
                          ln1_g[l][None, :], ln1_b[l][None, :], seq, alpha)
        x2 = _ffn(h2, x1, gate2, w_up[l].astype(BF16), w_down[l].astype(BF16),
                  ln2_g[l][None, :], ln2_b[l][None, :], seq, alpha)
    return x2.reshape(batch, seq, d)
```

```python
import functools

import jax
import jax.numpy as jnp
from jax import lax
from jax.experimental import pallas as pl
from jax.experimental.pallas import tpu as pltpu

F32 = jnp.float32
BF16 = jnp.bfloat16
HIGHEST = lax.Precision.HIGHEST

D_MODEL = 4096
D_SSM = 2048
D_SC = 2048
HEAD_DIM = 64
N_HEADS = 32
N_GROUPS = 8
HEADS_PER_GROUP = 4
D_STATE = 128
CHUNK = 128
SSM_CONV = 5
SC_CONV = 3
SC_GROUPS = 16
D_XBC = D_SSM + 2 * N_GROUPS * D_STATE
D_FF = 4 * D_MODEL
N_MOD = 6
LN_EPS = 1e-5
RMS_EPS = 1e-5
GROUP_W = HEADS_PER_GROUP * HEAD_DIM
D_MAIN = D_XBC + D_SSM + 3 * D_SC
DT_PAD = 128
HALO = 8
HALO_BF16 = 16
LANES = 128
LN_ROWS = 64
ACC_COLS = 512
X_PREFETCH_STEP = 4

COL_Z = 0
COL_H = 1
COL_B = 2
COL_C = 3

VMEM_LIMIT = 62 * 1024 * 1024


def _cparams(sem):
    return pltpu.CompilerParams(dimension_semantics=sem, vmem_limit_bytes=VMEM_LIMIT)


def _sigmoid(v):
    return 1.0 / (1.0 + jnp.exp(-v))


def _silu(v):
    return v * _sigmoid(v)


def _softplus(v):
    return jnp.maximum(v, 0.0) + jnp.log1p(jnp.exp(-jnp.abs(v)))


def _layer_norm(r, g, b):
    mu = jnp.mean(r, axis=-1, keepdims=True)
    d = r - mu
    var = jnp.mean(d * d, axis=-1, keepdims=True)
    return d * lax.rsqrt(var + LN_EPS) * g + b


def _ada_body(c_ref, w_ref, b_ref, o_ref):
    o_ref[...] = jnp.dot(_silu(c_ref[...]).astype(BF16), w_ref[...].astype(BF16),
                         preferred_element_type=F32) + b_ref[...]


def _ada(c_pad, w_ada, b_ada):
    rows, d = c_pad.shape
    n = w_ada.shape[1]
    tn = 512
    return pl.pallas_call(
        _ada_body,
        grid=(n // tn,),
        in_specs=[pl.BlockSpec((rows, d), lambda j: (0, 0)),
                  pl.BlockSpec((d, tn), lambda j: (0, j)),
                  pl.BlockSpec((1, tn), lambda j: (0, j))],
        out_specs=pl.BlockSpec((rows, tn), lambda j: (0, j)),
        out_shape=jax.ShapeDtypeStruct((rows, n), F32),
        compiler_params=_cparams(("arbitrary",)),
        name="ada",
    )(c_pad, w_ada, b_ada)


def _dot_nt(a, b_t):
    return lax.dot_general(a, b_t, (((1,), (1,)), ((), ())), preferred_element_type=F32)


def _x_tile_copy(x_hbm, x_bufs, sems, tile, tm):
    rows = pl.ds(pl.multiple_of(tile * tm, tm), tm)
    slot = tile % 2
    return pltpu.make_async_copy(x_hbm.at[rows, :], x_bufs.at[slot], sems.at[slot])


def _inproj_body(x_hbm, sc_ref, sh_ref, w_ref, wdt_ref, xbc_ref, rest_ref, dt_ref, h_scr, x_bufs, x_sems,
                 *, xbc_tiles, tm):
    i = pl.program_id(0)
    j = pl.program_id(1)

    @pl.when((i == 0) & (j == 0))
    def _():
        _x_tile_copy(x_hbm, x_bufs, x_sems, 0, tm).start()

    @pl.when((j == X_PREFETCH_STEP) & (i + 1 < pl.num_programs(0)))
    def _():
        _x_tile_copy(x_hbm, x_bufs, x_sems, i + 1, tm).start()

    @pl.when(j == 0)
    def _():
        _x_tile_copy(x_hbm, x_bufs, x_sems, i, tm).wait()
        h = (x_bufs[i % 2] * (1.0 + sc_ref[0]) + sh_ref[0]).astype(BF16)
        h_scr[...] = h
        dt_ref[...] = _dot_nt(h, wdt_ref[...])

    @pl.when(j < xbc_tiles)
    def _():
        xbc_ref[...] = _dot_nt(h_scr[...], w_ref[...])

    @pl.when(j >= xbc_tiles)
    def _():
        rest_ref[...] = _dot_nt(h_scr[...], w_ref[...]).astype(BF16)


def _inproj(x2, scale, shift, w_t, w_dt_t, seq):
    m, d = x2.shape
    n = D_MAIN
    tm, tn = 1024, 512
    tiles_per_batch = seq // tm
    xbc_tiles = D_XBC // tn
    z_tiles = D_SSM // tn
    hbc_start = D_SSM + D_XBC + 2 * N_HEADS

    def w_row_start(j):
        start = jnp.where(j < xbc_tiles, D_SSM + j * tn,
                          jnp.where(j < xbc_tiles + z_tiles, (j - xbc_tiles) * tn,
                                    hbc_start + (j - xbc_tiles - z_tiles) * tn))
        return pl.multiple_of(start, 2 * N_HEADS)
    return pl.pallas_call(
        functools.partial(_inproj_body, xbc_tiles=xbc_tiles, tm=tm),
        grid=(m // tm, n // tn),
        in_specs=[pl.BlockSpec(memory_space=pl.ANY),
                  pl.BlockSpec((1, 1, d), lambda i, j: (i // tiles_per_batch, 0, 0)),
                  pl.BlockSpec((1, 1, d), lambda i, j: (i // tiles_per_batch, 0, 0)),
                  pl.BlockSpec((pl.Element(tn), pl.Element(d)), lambda i, j: (w_row_start(j), 0)),
                  pl.BlockSpec((DT_PAD, d), lambda i, j: (0, 0))],
        out_specs=[pl.BlockSpec((tm, tn), lambda i, j: (i, jnp.minimum(j, xbc_tiles - 1))),
                   pl.BlockSpec((tm, tn), lambda i, j: (i, jnp.maximum(j - xbc_tiles, 0))),
                   pl.BlockSpec((tm, DT_PAD), lambda i, j: (i, 0))],
        out_shape=[jax.ShapeDtypeStruct((m, D_XBC), F32),
                   jax.ShapeDtypeStruct((m, n - D_XBC), BF16),
                   jax.ShapeDtypeStruct((m, DT_PAD), F32)],
        scratch_shapes=[pltpu.VMEM((tm, d), BF16), pltpu.VMEM((2, tm, d), F32), pltpu.SemaphoreType.DMA((2,))],
        compiler_params=_cparams(("arbitrary", "arbitrary")),
        name="inproj",
    )(x2, scale, shift, w_t, w_dt_t)


def _expand_heads(q, e3_ref):
    hi = q.astype(BF16)
    r1 = q - hi.astype(F32)
    mid = r1.astype(BF16)
    lo = (r1 - mid.astype(F32)).astype(BF16)
    pieces = jnp.concatenate([hi, mid, lo], axis=1)
    return jnp.dot(pieces, e3_ref[...], preferred_element_type=F32)


def _chunk_decay_terms(dtq, alog_ref):
    row = lax.broadcasted_iota(jnp.int32, (CHUNK, CHUNK), 0)
    col = lax.broadcasted_iota(jnp.int32, (CHUNK, CHUNK), 1)
    tril = col <= row
    triu = col >= row
    da = dtq * (-jnp.exp(alog_ref[...]))
    pre = jnp.dot(tril.astype(F32), da, preferred_element_type=F32, precision=HIGHEST)
    suf = jnp.dot(triu.astype(F32), da, preferred_element_type=F32, precision=HIGHEST)
    acum = jnp.where(col < N_HEADS, pre, suf)
    return acum, tril, triu


def _ssd_bwd_body(cur_ref, prev_ref, next_ref, dt_ref, cw_ref, cb_ref, bias_ref, alog_ref, e3_ref,
                  act_ref, yoff_ref, dtq_ref, st_ref, *, n_chunks):
    s = pl.program_id(1)
    c = n_chunks - 1 - s

    @pl.when(s == 0)
    def _():
        st_ref[...] = jnp.zeros_like(st_ref)

    lane_tile = 512
    rows = CHUNK + 2 * HALO
    has_prev = c > 0
    has_next = c < n_chunks - 1
    for t in range(D_XBC // lane_tile):
        sl = slice(t * lane_tile, (t + 1) * lane_tile)
        stack = jnp.concatenate([jnp.where(has_prev, prev_ref[:, sl], 0.0), cur_ref[:, sl],
                                 jnp.where(has_next, next_ref[:, sl], 0.0)], axis=0)
        w = cw_ref[:, sl]
        conv = cb_ref[:, sl] + w[SSM_CONV // 2:SSM_CONV // 2 + 1, :] * stack[HALO:HALO + CHUNK, :]
        for k in range(SSM_CONV):
            shift = SSM_CONV // 2 - k
            if shift != 0:
                tap = pltpu.roll(stack, shift % rows, 0)[HALO:HALO + CHUNK, :]
                conv = conv + w[k:k + 1, :] * tap
        act_ref[:, sl] = _silu(conv)

    dtq = _softplus(dt_ref[...] + bias_ref[...])
    dtq_ref[...] = dtq
    acum, _, _ = _chunk_decay_terms(dtq, alog_ref)
    lane = lax.broadcasted_iota(jnp.int32, (CHUNK, DT_PAD), 1)
    acum = jnp.where((lane >= N_HEADS) & (lane < 2 * N_HEADS), acum, 0.0)
    total = acum[0:1, :]
    eb_x = _expand_heads(jnp.exp(acum), e3_ref)
    wb_x = _expand_heads(dtq * jnp.exp(total - acum), e3_ref)
    dec_x = eb_x[0:1, :]

    for g in range(N_GROUPS):
        ch = slice(g * GROUP_W, (g + 1) * GROUP_W)
        b_g = act_ref[:, D_SSM + g * D_STATE:D_SSM + (g + 1) * D_STATE].astype(BF16)
        c_g = act_ref[:, D_SSM + N_GROUPS * D_STATE + g * D_STATE:
                      D_SSM + N_GROUPS * D_STATE + (g + 1) * D_STATE].astype(BF16)
        st = st_ref[g]
        yoff_ref[:, ch] = jnp.dot(c_g, st.astype(BF16), preferred_element_type=F32) * eb_x[:, ch]
        xw = (act_ref[:, ch] * wb_x[:, ch]).astype(BF16)
        upd = lax.dot_general(b_g, xw, (((0,), (0,)), ((), ())), preferred_element_type=F32)
        st_ref[g] = st * dec_x[:, ch] + upd


def _ssd_bwd(p_main, dt, conv_w, conv_b, bias_q, alog_q, e3b, batch, seq):
    m = p_main.shape[0]
    nc = seq // CHUNK
    rb = CHUNK // HALO
    last_halo = m // HALO - 1

    def cur_map(b, s):
        return (b * nc + (nc - 1 - s), 0)

    def prev_map(b, s):
        return (jnp.maximum((b * nc + (nc - 1 - s)) * rb - 1, 0), 0)

    def next_map(b, s):
        return (jnp.minimum((b * nc + (nc - 1 - s) + 1) * rb, last_halo), 0)

    const = lambda b, s: (0, 0)
    return pl.pallas_call(
        functools.partial(_ssd_bwd_body, n_chunks=nc),
        grid=(batch, nc),
        in_specs=[pl.BlockSpec((CHUNK, D_XBC), cur_map),
                  pl.BlockSpec((HALO, D_XBC), prev_map),
                  pl.BlockSpec((HALO, D_XBC), next_map),
                  pl.BlockSpec((CHUNK, DT_PAD), cur_map),
                  pl.BlockSpec((HALO, D_XBC), const),
                  pl.BlockSpec((1, D_XBC), const),
                  pl.BlockSpec((1, DT_PAD), const),
                  pl.BlockSpec((1, DT_PAD), const),
                  pl.BlockSpec((3 * DT_PAD, D_SSM), const)],
        out_specs=[pl.BlockSpec((CHUNK, D_XBC), cur_map),
                   pl.BlockSpec((CHUNK, D_SSM), cur_map),
                   pl.BlockSpec((CHUNK, DT_PAD), cur_map)],
        out_shape=[jax.ShapeDtypeStruct((m, D_XBC), F32),
                   jax.ShapeDtypeStruct((m, D_SSM), F32),
                   jax.ShapeDtypeStruct((m, DT_PAD), F32)],
        scratch_shapes=[pltpu.VMEM((N_GROUPS, D_STATE, GROUP_W), F32)],
        compiler_params=_cparams(("arbitrary", "arbitrary")),
        name="ssd_bwd",
    )(p_main, p_main, p_main, dt, conv_w, conv_b, bias_q, alog_q, e3b)


def _ssd_fwd_body(act_ref, z_ref, dtq_ref, yoffb_ref, alog_ref, dskip_ref, normw_ref, e3_ref,
                  y_ref, st_ref):
    @pl.when(pl.program_id(1) == 0)
    def _():
        st_ref[...] = jnp.zeros_like(st_ref)

    dtq = dtq_ref[...]
    acum, tril, triu = _chunk_decay_terms(dtq, alog_ref)
    acum_t = acum.T
    dtq_t = dtq.T
    lane = lax.broadcasted_iota(jnp.int32, (CHUNK, DT_PAD), 1)
    acum_f = jnp.where(lane < N_HEADS, acum, 0.0)
    total = acum_f[CHUNK - 1:CHUNK, :]
    ef_x = _expand_heads(jnp.exp(acum_f), e3_ref)
    wf_x = _expand_heads(dtq * jnp.exp(total - acum_f), e3_ref)
    dec_x = ef_x[CHUNK - 1:CHUNK, :]
    head_of_lane = lax.broadcasted_iota(jnp.int32, (CHUNK, GROUP_W), 1) // HEAD_DIM
    neg_inf = jnp.float32(-jnp.inf)

    for g in range(N_GROUPS):
        ch = slice(g * GROUP_W, (g + 1) * GROUP_W)
        b_g = act_ref[:, D_SSM + g * D_STATE:D_SSM + (g + 1) * D_STATE].astype(BF16)
        c_g = act_ref[:, D_SSM + N_GROUPS * D_STATE + g * D_STATE:
                      D_SSM + N_GROUPS * D_STATE + (g + 1) * D_STATE].astype(BF16)
        x_g = act_ref[:, ch]
        scores = lax.dot_general(c_g, b_g, (((1,), (1,)), ((), ())), preferred_element_type=F32)

        mats = []
        blocks = []
        for r in range(HEADS_PER_GROUP):
            hf = g * HEADS_PER_GROUP + r
            hb = N_HEADS + hf
            seg_f = acum[:, hf:hf + 1] - acum_t[hf:hf + 1, :]
            seg_b = acum[:, hb:hb + 1] - acum_t[hb:hb + 1, :]
            lf = jnp.exp(jnp.where(tril, seg_f, neg_inf)) * dtq_t[hf:hf + 1, :]
            ub = jnp.exp(jnp.where(triu, seg_b, neg_inf)) * dtq_t[hb:hb + 1, :]
            mats.append((scores * (lf + ub)).astype(BF16))
            blocks.append(jnp.where(head_of_lane == r, x_g, 0.0).astype(BF16))
        y_diag = jnp.dot(jnp.concatenate(mats, axis=1), jnp.concatenate(blocks, axis=0),
                         preferred_element_type=F32)

        st = st_ref[g]
        y_off = jnp.dot(c_g, st.astype(BF16), preferred_element_type=F32) * ef_x[:, ch]
        y = y_diag + y_off + yoffb_ref[:, ch] + dskip_ref[:, ch] * x_g

        xw = (x_g * wf_x[:, ch]).astype(BF16)
        upd = lax.dot_general(b_g, xw, (((0,), (0,)), ((), ())), preferred_element_type=F32)
        st_ref[g] = st * dec_x[:, ch] + upd

        y = y * _silu(z_ref[:, ch].astype(F32))
        ms = jnp.mean(y * y, axis=-1, keepdims=True)
        y_ref[:, ch] = (y * lax.rsqrt(ms + RMS_EPS) * normw_ref[:, ch]).astype(BF16)


def _ssd_fwd(act, p_rest, dtq, yoffb, alog_q, dskip_x, norm_w, e3f, batch, seq):
    m = act.shape[0]
    nc = seq // CHUNK
    cur = lambda b, s: (b * nc + s, 0)
    const = lambda b, s: (0, 0)
    return pl.pallas_call(
        _ssd_fwd_body,
        grid=(batch, nc),
        in_specs=[pl.BlockSpec((CHUNK, D_XBC), cur),
                  pl.BlockSpec((CHUNK, D_SSM), lambda b, s: (b * nc + s, COL_Z)),
                  pl.BlockSpec((CHUNK, DT_PAD), cur),
                  pl.BlockSpec((CHUNK, D_SSM), cur),
                  pl.BlockSpec((1, DT_PAD), const),
                  pl.BlockSpec((1, D_SSM), const),
                  pl.BlockSpec((1, D_SSM), const),
                  pl.BlockSpec((3 * DT_PAD, D_SSM), const)],
        out_specs=pl.BlockSpec((CHUNK, D_SSM), cur),
        out_shape=jax.ShapeDtypeStruct((m, D_SSM), BF16),
        scratch_shapes=[pltpu.VMEM((N_GROUPS, D_STATE, GROUP_W), F32)],
        compiler_params=_cparams(("arbitrary", "arbitrary")),
        name="ssd_fwd",
    )(act, p_rest, dtq, yoffb, alog_q, dskip_x, norm_w, e3f)


def _shortconv_body(h_ref, b_ref, c_ref, hp_ref, cp_ref, hn_ref, cn_ref, cw_ref, nw_ref, y_ref,
                    *, tiles_per_seq, rows):
    i = pl.program_id(0)
    t = i % tiles_per_seq
    has_prev = t > 0
    has_next = t < tiles_per_seq - 1
    lane_tile = 512
    row = lax.broadcasted_iota(jnp.int32, (rows, lane_tile), 0)
    group = D_SC // SC_GROUPS
    for q in range(D_SC // lane_tile):
        sl = slice(q * lane_tile, (q + 1) * lane_tile)
        v = c_ref[:, sl].astype(F32) * h_ref[:, sl].astype(F32)
        last = HALO_BF16 - 1
        vp = jnp.where(has_prev, cp_ref[:, sl].astype(F32)[last:, :] * hp_ref[:, sl].astype(F32)[last:, :], 0.0)
        vn = jnp.where(has_next, cn_ref[:, sl].astype(F32)[0:1, :] * hn_ref[:, sl].astype(F32)[0:1, :], 0.0)
        down = jnp.where(row == 0, vp, pltpu.roll(v, 1, 0))
        up = jnp.where(row == rows - 1, vn, pltpu.roll(v, rows - 1, 0))
        w = cw_ref[:, sl]
        y = b_ref[:, sl].astype(F32) * (w[0:1, :] * down + w[1:2, :] * v + w[2:3, :] * up)
        for k in range(lane_tile // group):
            gs = slice(k * group, (k + 1) * group)
            yg = y[:, gs]
            ms = jnp.mean(yg * yg, axis=-1, keepdims=True)
            lo = q * lane_tile + k * group
            y_ref[:, lo:lo + group] = (yg * lax.rsqrt(ms + RMS_EPS) * nw_ref[:, lo:lo + group]).astype(BF16)


def _shortconv(p_rest, conv_w, norm_w, seq):
    m = p_rest.shape[0]
    rows = 256
    tiles_per_seq = seq // rows
    rb = rows // HALO_BF16
    last_halo = m // HALO_BF16 - 1

    def cur(col):
        return lambda i: (i, col)

    def prev(col):
        return lambda i: (jnp.maximum(i * rb - 1, 0), col)

    def nxt(col):
        return lambda i: (jnp.minimum((i + 1) * rb, last_halo), col)

    const = lambda i: (0, 0)
    return pl.pallas_call(
        functools.partial(_shortconv_body, tiles_per_seq=tiles_per_seq, rows=rows),
        grid=(m // rows,),
        in_specs=[pl.BlockSpec((rows, D_SC), cur(COL_H)),
                  pl.BlockSpec((rows, D_SC), cur(COL_B)),
                  pl.BlockSpec((rows, D_SC), cur(COL_C)),
                  pl.BlockSpec((HALO_BF16, D_SC), prev(COL_H)),
                  pl.BlockSpec((HALO_BF16, D_SC), prev(COL_C)),
                  pl.BlockSpec((HALO_BF16, D_SC), nxt(COL_H)),
                  pl.BlockSpec((HALO_BF16, D_SC), nxt(COL_C)),
                  pl.BlockSpec((HALO, D_SC), const),
                  pl.BlockSpec((1, D_SC), const)],
        out_specs=pl.BlockSpec((rows, D_SC), lambda i: (i, 0)),
        out_shape=jax.ShapeDtypeStruct((m, D_SC), BF16),
        compiler_params=_cparams(("arbitrary",)),
        name="shortconv",
    )(p_rest, p_rest, p_rest, p_rest, p_rest, p_rest, p_rest, conv_w, norm_w)


def _residual_layer_norm(o_ref, x_ref, gate_ref, lng_ref, lnb_ref, alpha, modulated=None, on_slab=None):
    width = o_ref.shape[1]
    lane_tiles = [slice(t, t + LANES) for t in range(0, width, LANES)]

    def slab(i, carry):
        sl = pl.ds(pl.multiple_of(i * LN_ROWS, LN_ROWS), LN_ROWS)
        acc = jnp.zeros((LN_ROWS, LANES), F32)
        for lt in lane_tiles:
            r = alpha * x_ref[sl, lt] + (1.0 + gate_ref[0, :, lt]) * o_ref[sl, lt]
            o_ref[sl, lt] = r
            acc = acc + r
        mean = jnp.broadcast_to(jnp.sum(acc, axis=-1, keepdims=True) * (1.0 / width), (LN_ROWS, LANES))
        acc = jnp.zeros((LN_ROWS, LANES), F32)
        for lt in lane_tiles:
            d = o_ref[sl, lt] - mean
            acc = acc + d * d
        var = jnp.sum(acc, axis=-1, keepdims=True) * (1.0 / width)
        rstd = jnp.broadcast_to(lax.rsqrt(var + LN_EPS), (LN_ROWS, LANES))
        for lt in lane_tiles:
            y = (o_ref[sl, lt] - mean) * rstd * lng_ref[:, lt] + lnb_ref[:, lt]
            o_ref[sl, lt] = y
            if modulated is not None:
                h_ref, sc_ref, sh_ref = modulated
                h_ref[sl, lt] = (y * (1.0 + sc_ref[0, :, lt]) + sh_ref[0, :, lt]).astype(BF16)
        if on_slab is not None:
            on_slab(i)
        return carry

    lax.fori_loop(0, o_ref.shape[0] // LN_ROWS, slab, 0)


def _row_tile_copy(hbm_ref, buf_ref, sem, tm):
    rows = pl.ds(pl.multiple_of(pl.program_id(0) * tm, tm), tm)
    return pltpu.make_async_copy(hbm_ref.at[rows, :], buf_ref, sem)


def _slab_writeback(buf_ref, out_hbm, sems, tm, i):
    src = pl.ds(pl.multiple_of(i * LN_ROWS, LN_ROWS), LN_ROWS)
    dst = pl.ds(pl.multiple_of(pl.program_id(0) * tm + i * LN_ROWS, LN_ROWS), LN_ROWS)
    return pltpu.make_async_copy(buf_ref.at[src, :], out_hbm.at[dst, :], sems.at[i])


def _outproj_body(ys_ref, yc_ref, w_ref, x_hbm, gate_ref, sc_ref, sh_ref, lng_ref, lnb_ref, x1_hbm, h2_hbm,
                  x1_buf, h2_buf, x_buf, sem, x1_sems, h2_sems, *, alpha, tm, tn):
    n = pl.program_id(1)

    @pl.when(n == 1)
    def _():
        _row_tile_copy(x_hbm, x_buf, sem, tm).start()

    k_ssm = ys_ref.shape[1]
    cols = pl.ds(pl.multiple_of(n * tn, tn), tn)
    x1_buf[:, cols] = (jnp.dot(ys_ref[...], w_ref[:k_ssm, :], preferred_element_type=F32)
                       + jnp.dot(yc_ref[...], w_ref[k_ssm:, :], preferred_element_type=F32))

    @pl.when(n == pl.num_programs(1) - 1)
    def _():
        _row_tile_copy(x_hbm, x_buf, sem, tm).wait()

        def flush(i):
            _slab_writeback(x1_buf, x1_hbm, x1_sems, tm, i).start()
            _slab_writeback(h2_buf, h2_hbm, h2_sems, tm, i).start()

        _residual_layer_norm(x1_buf, x_buf, gate_ref, lng_ref, lnb_ref, alpha,
                             modulated=(h2_buf, sc_ref, sh_ref), on_slab=flush)
        for i in range(tm // LN_ROWS):
            _slab_writeback(x1_buf, x1_hbm, x1_sems, tm, i).wait()
            _slab_writeback(h2_buf, h2_hbm, h2_sems, tm, i).wait()


def _outproj(y_ssm, y_sc, w_out, x2, gate, scale2, shift2, ln_g, ln_b, seq, alpha):
    m, d = x2.shape
    tm, tn = 512, 1024
    tiles_per_batch = seq // tm
    mod_map = lambda i, n: (i // tiles_per_batch, 0, 0)
    const = lambda i, n: (0, 0)
    row_tile = lambda i, n: (i, 0)
    return pl.pallas_call(
        functools.partial(_outproj_body, alpha=alpha, tm=tm, tn=tn),
        grid=(m // tm, d // tn),
        in_specs=[pl.BlockSpec((tm, D_SSM), row_tile),
                  pl.BlockSpec((tm, D_SC), row_tile),
                  pl.BlockSpec((D_SSM + D_SC, tn), lambda i, n: (0, n)),
                  pl.BlockSpec(memory_space=pl.ANY),
                  pl.BlockSpec((1, 1, d), mod_map),
                  pl.BlockSpec((1, 1, d), mod_map),
                  pl.BlockSpec((1, 1, d), mod_map),
                  pl.BlockSpec((1, d), const),
                  pl.BlockSpec((1, d), const)],
        out_specs=[pl.BlockSpec(memory_space=pl.ANY),
                   pl.BlockSpec(memory_space=pl.ANY)],
        out_shape=[jax.ShapeDtypeStruct((m, d), F32),
                   jax.ShapeDtypeStruct((m, d), BF16)],
        scratch_shapes=[pltpu.VMEM((tm, d), F32), pltpu.VMEM((tm, d), BF16), pltpu.VMEM((tm, d), F32),
                        pltpu.SemaphoreType.DMA(()), pltpu.SemaphoreType.DMA((tm // LN_ROWS,)),
                        pltpu.SemaphoreType.DMA((tm // LN_ROWS,))],
        compiler_params=_cparams(("arbitrary", "arbitrary")),
        name="outproj",
    )(y_ssm, y_sc, w_out, x2, gate, scale2, shift2, ln_g, ln_b)


def _ffn_body(h2_ref, x1_hbm, gate_ref, wu_ref, wd_ref, lng_ref, lnb_ref, out_hbm, acc_ref, x1_buf, sem,
              out_sems, *, alpha, tm):
    f = pl.program_id(1)

    @pl.when(f == 0)
    def _():
        acc_ref[...] = jnp.zeros_like(acc_ref)

    @pl.when(f == 1)
    def _():
        _row_tile_copy(x1_hbm, x1_buf, sem, tm).start()

    u = jnp.maximum(jnp.dot(h2_ref[...], wu_ref[...], preferred_element_type=F32), 0.0)
    u = (u * u).astype(BF16)
    for n0 in range(0, acc_ref.shape[1], ACC_COLS):
        cols = slice(n0, n0 + ACC_COLS)
        acc_ref[:, cols] += jnp.dot(u, wd_ref[:, cols], preferred_element_type=F32)

    @pl.when(f == pl.num_programs(1) - 1)
    def _():
        _row_tile_copy(x1_hbm, x1_buf, sem, tm).wait()
        _residual_layer_norm(acc_ref, x1_buf, gate_ref, lng_ref, lnb_ref, alpha,
                             on_slab=lambda i: _slab_writeback(acc_ref, out_hbm, out_sems, tm, i).start())
        for i in range(tm // LN_ROWS):
            _slab_writeback(acc_ref, out_hbm, out_sems, tm, i).wait()


def _ffn(h2, x1, gate, w_up, w_down, ln_g, ln_b, seq, alpha):
    m, d = x1.shape
    ff = w_up.shape[1]
    tm, tf = 512, 1024
    tiles_per_batch = seq // tm
    mod_map = lambda i, f: (i // tiles_per_batch, 0, 0)
    const = lambda i, f: (0, 0)
    row_tile = lambda i, f: (i, 0)
    return pl.pallas_call(
        functools.partial(_ffn_body, alpha=alpha, tm=tm),
        grid=(m // tm, ff // tf),
        in_specs=[pl.BlockSpec((tm, d), row_tile),
                  pl.BlockSpec(memory_space=pl.ANY),
                  pl.BlockSpec((1, 1, d), mod_map),
                  pl.BlockSpec((d, tf), lambda i, f: (0, f)),
                  pl.BlockSpec((tf, d), lambda i, f: (f, 0)),
                  pl.BlockSpec((1, d), const),
                  pl.BlockSpec((1, d), const)],
        out_specs=pl.BlockSpec(memory_space=pl.ANY),
        out_shape=jax.ShapeDtypeStruct((m, d), F32),
        scratch_shapes=[pltpu.VMEM((tm, d), F32), pltpu.VMEM((tm, d), F32), pltpu.SemaphoreType.DMA(()),
                        pltpu.SemaphoreType.DMA((tm // LN_ROWS,))],
        compiler_params=_cparams(("arbitrary", "arbitrary")),
        name="ffn",
    )(h2, x1, gate, w_up, w_down, ln_g, ln_b)


def _head_expansion(lane_offset):
    rows = jnp.arange(DT_PAD)[:, None]
    heads = jnp.arange(D_SSM)[None, :] // HEAD_DIM
    e = (rows == heads + lane_offset).astype(BF16)
    return jnp.tile(e, (3, 1))


def _pad_rows(a, rows):
    return jnp.pad(a, ((0, rows - a.shape[0]), (0, 0)))


def kernel(x, c, w_ada, b_ada, w_in, ssm_conv_w, ssm_conv_b, ssm_dt_bias_f, ssm_dt_bias_b, ssm_a_log_f, ssm_a_log_b, ssm_d, ssm_norm_w, sc_conv_w, sc_norm_w, w_out, ln1_g, ln1_b, w_up, w_down, ln2_g, ln2_b):
    batch, seq, d = x.shape
    depth = w_ada.shape[0]
    alpha = float((2 * depth) ** 0.25)
    m = batch * seq
    dt_lo = D_SSM + D_XBC
    dt_hi = dt_lo + 2 * N_HEADS
    e3f = _head_expansion(0)
    e3b = _head_expansion(N_HEADS)
    c_pad = _pad_rows(c, HALO)
    x2 = x.reshape(m, d)

    for l in range(depth):
        mod = _ada(c_pad, w_ada[l], b_ada[l][None, :])[:batch]
        shift1, scale1, gate1, shift2, scale2, gate2 = [
            t[:, None, :] for t in jnp.split(mod, N_MOD, axis=-1)]

        w_t = w_in[l].T.astype(BF16)
        w_dt_t = jnp.pad(w_t[dt_lo:dt_hi], ((0, DT_PAD - 2 * N_HEADS), (0, 0)))
        p_xbc, p_rest, dt = _inproj(x2, scale1, shift1, w_t, w_dt_t, seq)

        zeros_q = jnp.zeros((DT_PAD - 2 * N_HEADS,), F32)
        bias_q = jnp.concatenate([ssm_dt_bias_f[l], ssm_dt_bias_b[l], zeros_q])[None, :]
        alog_q = jnp.concatenate([ssm_a_log_f[l], ssm_a_log_b[l], zeros_q])[None, :]
        act, yoffb, dtq = _ssd_bwd(p_xbc, dt, _pad_rows(ssm_conv_w[l], HALO), ssm_conv_b[l][None, :],
                                   bias_q, alog_q, e3b, batch, seq)
        dskip_x = jnp.repeat(ssm_d[l], HEAD_DIM)[None, :]
        y_ssm = _ssd_fwd(act, p_rest, dtq, yoffb, alog_q, dskip_x, ssm_norm_w[l][None, :], e3f, batch, seq)
        y_sc = _shortconv(p_rest, _pad_rows(sc_conv_w[l], HALO), sc_norm_w[l][None, :], seq)

        x1, h2 = _outproj(y_ssm, y_sc, w_out[l].astype(BF16), x2, gate1, scale2, shift2,
                          ln1_g[l][None, :], ln1_b[l][None, :], seq, alpha)
        x2 = _ffn(h2, x1, gate2, w_up[l].astype(BF16), w_down[l].astype(BF16),
                  ln2_g[l][None, :], ln2_b[l][None, :], seq, alpha)
    return x2.reshape(batch, seq, d)
```
